```python
import math
import jax
import jax.numpy as jnp
from jax import lax
import numpy as np

D_MODEL = 1024
BATCH = 2
SEQ = 8192
DEPTH = 2

SSM_HEADS = 16
SSM_HEAD_DIM = 64
SSM_INNER = SSM_HEADS * SSM_HEAD_DIM
SSM_GROUPS = 2
SSM_STATE = 128
SSM_CONV = 4
SSM_CHUNK = 128
SSM_CONV_DIM = SSM_INNER + 2 * SSM_GROUPS * SSM_STATE

GMLP_GROUPS = 8
GMLP_GROUP_DIM = 128
GMLP_INNER = GMLP_GROUPS * GMLP_GROUP_DIM
GMLP_CHUNK = 128

MIX_WIDTH = SSM_INNER + GMLP_INNER
IN_EVEN = SSM_INNER + SSM_CONV_DIM + SSM_HEADS + 2 * GMLP_INNER

ATTN_HEADS = 16
ATTN_KV_HEADS = 2
ATTN_HEAD_DIM = 64
ATTN_Q_PER_KV = ATTN_HEADS // ATTN_KV_HEADS
WINDOW = 128
ATTN_BLOCK = 128
QKV_DIM = (ATTN_HEADS + 2 * ATTN_KV_HEADS) * ATTN_HEAD_DIM
REL_BUCKETS = 32
REL_MAX_DIST = 128

FFN_HIDDEN = -(-8 * D_MODEL // (3 * 256)) * 256

N_EVEN = (DEPTH + 1) // 2
N_ODD = DEPTH // 2
EPS = 1e-6
NEG_INF = -1e30

kernel_name = "hybrid_ssd_gmlp_swa_adaln_trunk"


def rms_norm(x, w):
    xf = x.astype(jnp.float32)
    y = xf * lax.rsqrt(jnp.mean(xf * xf, axis=-1, keepdims=True) + EPS)
    return (y * w.astype(jnp.float32)).astype(x.dtype)


def layer_norm(x, w, b):
    xf = x.astype(jnp.float32)
    mu = jnp.mean(xf, axis=-1, keepdims=True)
    var = jnp.mean(jnp.square(xf - mu), axis=-1, keepdims=True)
    y = (xf - mu) * lax.rsqrt(var + EPS)
    return (y * w.astype(jnp.float32) + b.astype(jnp.float32)).astype(x.dtype)


def modulate(h, shift, scale):
    return h * (1 + scale[:, None, :]) + shift[:, None, :]


def causal_dwconv(x, w, b):
    out = lax.conv_general_dilated(
        x, w[:, None, :].astype(x.dtype), window_strides=(1,),
        padding=[(SSM_CONV - 1, 0)], dimension_numbers=('NWC', 'WIO', 'NWC'),
        feature_group_count=x.shape[-1])
    return out + b


def ssd_chunked(x, dt, a, bmat, cmat):
    f32 = jnp.float32
    bsz, seq, nh, hd = x.shape
    ng, ns = bmat.shape[-2:]
    hpg = nh // ng
    nc = seq // SSM_CHUNK
    L = SSM_CHUNK
    xd = (x.astype(f32) * dt[..., None]).reshape(bsz, nc, L, ng, hpg, hd)
    la = jnp.moveaxis((dt * a).reshape(bsz, nc, L, ng, hpg), 2, -1)
    la_cum = jnp.cumsum(la, axis=-1)
    bc = bmat.astype(f32).reshape(bsz, nc, L, ng, ns)
    cc = cmat.astype(f32).reshape(bsz, nc, L, ng, ns)
    causal = jnp.tril(jnp.ones((L, L), dtype=bool))
    seg = la_cum[..., :, None] - la_cum[..., None, :]
    decay = jnp.where(causal, jnp.exp(jnp.where(causal, seg, 0.0)), 0.0)
    cb = jnp.einsum('bclgn,bcsgn->bcgls', cc, bc)
    y_diag = jnp.einsum('bcgjls,bcsgjp->bclgjp', cb[:, :, :, None] * decay, xd)
    decay_to_end = jnp.exp(la_cum[..., -1:] - la_cum)
    states = jnp.einsum('bcsgn,bcgjs,bcsgjp->bcgjpn', bc, decay_to_end, xd)
    chunk_decay = jnp.exp(la_cum[..., -1])

    def step(carry, inp):
        st, dec = inp
        return carry * dec[..., None, None] + st, carry

    init = jnp.zeros((bsz, ng, hpg, hd, ns), f32)
    _, prev = lax.scan(step, init, (jnp.moveaxis(states, 1, 0), jnp.moveaxis(chunk_decay, 1, 0)))
    prev = jnp.moveaxis(prev, 0, 1)
    y_off = jnp.einsum('bclgn,bcgjpn,bcgjl->bclgjp', cc, prev, jnp.exp(la_cum))
    return (y_diag + y_off).reshape(bsz, seq, nh, hd)


def ssd_branch(z, xbc, dt_raw, conv_w, conv_b, dt_bias, a_log, d_skip, norm_w):
    f32 = jnp.float32
    bsz, seq, _ = z.shape
    xbc = jax.nn.silu(causal_dwconv(xbc, conv_w, conv_b))
    xs, bm, cm = jnp.split(xbc, [SSM_INNER, SSM_INNER + SSM_GROUPS * SSM_STATE], axis=-1)
    xs = xs.reshape(bsz, seq, SSM_HEADS, SSM_HEAD_DIM)
    bm = bm.reshape(bsz, seq, SSM_GROUPS, SSM_STATE)
    cm = cm.reshape(bsz, seq, SSM_GROUPS, SSM_STATE)
    dt = jax.nn.softplus(dt_raw.astype(f32) + dt_bias.astype(f32))
    a = -jnp.exp(a_log.astype(f32))
    y = ssd_chunked(xs, dt, a, bm, cm) + d_skip.astype(f32)[:, None] * xs.astype(f32)
    y = y.reshape(bsz, seq, SSM_INNER) * jax.nn.silu(z.astype(f32))
    y = y.reshape(bsz, seq, SSM_GROUPS, SSM_INNER // SSM_GROUPS)
    y = y * lax.rsqrt(jnp.mean(y * y, axis=-1, keepdims=True) + EPS)
    return (y.reshape(bsz, seq, SSM_INNER) * norm_w.astype(f32)).astype(z.dtype)


def spatial_gating_branch(u, v, ln_w, ln_b, w_s, b_s):
    bsz, seq, _ = u.shape
    nc = seq // GMLP_CHUNK
    u = jax.nn.gelu(u, approximate=False)
    v = layer_norm(jax.nn.gelu(v, approximate=False), ln_w, ln_b)
    v = v.reshape(bsz, nc, GMLP_CHUNK, GMLP_GROUPS, GMLP_GROUP_DIM)
    w = w_s * jnp.tril(jnp.ones((GMLP_CHUNK, GMLP_CHUNK), w_s.dtype))
    sv = jnp.einsum('gts,bcsgd->bctgd', w, v) + b_s.T[None, None, :, :, None]
    return u * sv.reshape(bsz, seq, GMLP_INNER)


def even_mixer(h, in_w, conv_w, conv_b, dt_bias, a_log, d_skip, ssm_norm_w,
               ln_w, ln_b, w_s, b_s, out_w):
    proj = h @ in_w
    o1 = SSM_INNER
    o2 = o1 + SSM_CONV_DIM
    o3 = o2 + SSM_HEADS
    o4 = o3 + GMLP_INNER
    z, xbc, dt_raw, u, v = jnp.split(proj, [o1, o2, o3, o4], axis=-1)
    ya = ssd_branch(z, xbc, dt_raw, conv_w, conv_b, dt_bias, a_log, d_skip, ssm_norm_w)
    yb = spatial_gating_branch(u, v, ln_w, ln_b, w_s, b_s)
    return jnp.concatenate([ya, yb], axis=-1) @ out_w


def t5_relative_bias(table):
    qi = jnp.arange(ATTN_BLOCK)[:, None]
    sj = jnp.arange(2 * ATTN_BLOCK)[None, :]
    dist = jnp.maximum(qi + ATTN_BLOCK - sj, 0)
    max_exact = REL_BUCKETS // 2
    log_ratio = (jnp.log(jnp.maximum(dist, 1).astype(jnp.float32) / max_exact)
                 / math.log(REL_MAX_DIST / max_exact))
    large = max_exact + (log_ratio * (REL_BUCKETS - max_exact)).astype(jnp.int32)
    bucket = jnp.where(dist < max_exact, dist, jnp.minimum(large, REL_BUCKETS - 1))
    bias = table[bucket]
    return jnp.transpose(bias, (2, 0, 1)).reshape(
        ATTN_KV_HEADS, ATTN_Q_PER_KV, ATTN_BLOCK, 2 * ATTN_BLOCK)


def window_attention(h, w_qkv, b_qkv, w_o, b_o, sinks, rel_table):
    f32 = jnp.float32
    bsz, seq, _ = h.shape
    nb = seq // ATTN_BLOCK
    qkv = h @ w_qkv + b_qkv
    q, k, v = jnp.split(qkv, [ATTN_HEADS * ATTN_HEAD_DIM,
                              (ATTN_HEADS + ATTN_KV_HEADS) * ATTN_HEAD_DIM], axis=-1)
    q = q.reshape(bsz, nb, ATTN_BLOCK, ATTN_KV_HEADS, ATTN_Q_PER_KV, ATTN_HEAD_DIM)

    def band(t):
        t = t.reshape(bsz, nb, ATTN_BLOCK, ATTN_KV_HEADS, ATTN_HEAD_DIM)
        prev = jnp.pad(t, ((0, 0), (1, 0), (0, 0), (0, 0), (0, 0)))[:, :-1]
        return jnp.concatenate([prev, t], axis=2)

    kb, vb = band(k), band(v)
    scale = ATTN_HEAD_DIM ** -0.5
    logits = jnp.einsum('bnqkgd,bnskd->bnkgqs', q.astype(f32), kb.astype(f32)) * scale
    logits = logits + t5_relative_bias(rel_table).astype(f32)
    qi = jnp.arange(ATTN_BLOCK)[:, None]
    sj = jnp.arange(2 * ATTN_BLOCK)[None, :]
    rel = qi + ATTN_BLOCK - sj
    in_window = (rel >= 0) & (rel < WINDOW)
    key_pos = jnp.arange(nb)[:, None, None] * ATTN_BLOCK - ATTN_BLOCK + sj[None]
    mask = in_window[None] & (key_pos >= 0)
    logits = jnp.where(mask[None, :, None, None], logits, NEG_INF)
    sink = jnp.broadcast_to(
        sinks.astype(f32).reshape(ATTN_KV_HEADS, ATTN_Q_PER_KV)[None, None, :, :, None, None],
        logits.shape[:-1] + (1,))
    probs = jax.nn.softmax(jnp.concatenate([logits, sink], axis=-1), axis=-1)[..., :-1]
    out = jnp.einsum('bnkgqs,bnskd->bnqkgd', probs.astype(vb.dtype), vb)
    return out.reshape(bsz, seq, ATTN_HEADS * ATTN_HEAD_DIM) @ w_o + b_o


def swiglu(h, w_gate, w_up, w_down):
    return (jax.nn.silu(h @ w_gate) * (h @ w_up)) @ w_down


def setup_inputs(seed: int = 0) -> dict:
    key = jax.random.key(seed)
    ks = iter(jax.random.split(key, 40))
    nrm = lambda shape, s: jax.random.normal(next(ks), shape, jnp.float32) * s
    ones_n = lambda shape: 1.0 + nrm(shape, 0.02)
    D = D_MODEL
    dt0 = jnp.exp(jax.random.uniform(next(ks), (N_EVEN, SSM_HEADS), jnp.float32,
                                     math.log(1e-3), math.log(1e-1)))
    return {
        "x": nrm((BATCH, SEQ, D), 1.0),
        "c": nrm((BATCH, D), 1.0),
        "ada_w": nrm((DEPTH, D, 6 * D), 0.5 * D ** -0.5),
        "ada_b": nrm((DEPTH, 6 * D), 0.02),
        "norm_mix_w": ones_n((DEPTH, D)),
        "norm_ffn_w": ones_n((DEPTH, D)),
        "in_w_even": nrm((N_EVEN, D, IN_EVEN), D ** -0.5),
        "conv_w": nrm((N_EVEN, SSM_CONV, SSM_CONV_DIM), SSM_CONV ** -0.5),
        "conv_b": nrm((N_EVEN, SSM_CONV_DIM), 0.02),
        "dt_bias": dt0 + jnp.log(-jnp.expm1(-dt0)),
        "a_log": jnp.log(jax.random.uniform(next(ks), (N_EVEN, SSM_HEADS), jnp.float32, 1.0, 16.0)),
        "d_skip": 1.0 + nrm((N_EVEN, SSM_HEADS), 0.1),
        "ssm_norm_w": ones_n((N_EVEN, SSM_INNER)),
        "gmlp_ln_w": ones_n((N_EVEN, GMLP_INNER)),
        "gmlp_ln_b": nrm((N_EVEN, GMLP_INNER), 0.02),
        "gmlp_ws": nrm((N_EVEN, GMLP_GROUPS, GMLP_CHUNK, GMLP_CHUNK), GMLP_CHUNK ** -0.5),
        "gmlp_bs": 1.0 + nrm((N_EVEN, GMLP_GROUPS, GMLP_CHUNK), 0.02),
        "out_w_even": nrm((N_EVEN, MIX_WIDTH, D), MIX_WIDTH ** -0.5),
        "qkv_w": nrm((N_ODD, D, QKV_DIM), D ** -0.5),
        "qkv_b": nrm((N_ODD, QKV_DIM), 0.02),
        "o_w": nrm((N_ODD, ATTN_HEADS * ATTN_HEAD_DIM, D), (ATTN_HEADS * ATTN_HEAD_DIM) ** -0.5),
        "o_b": nrm((N_ODD, D), 0.02),
        "sinks": nrm((N_ODD, ATTN_HEADS), 1.0),
        "rel_table": nrm((REL_BUCKETS, ATTN_HEADS), 0.5),
        "ffn_gate_w": nrm((DEPTH, D, FFN_HIDDEN), D ** -0.5),
        "ffn_up_w": nrm((DEPTH, D, FFN_HIDDEN), D ** -0.5),
        "ffn_down_w": nrm((DEPTH, FFN_HIDDEN, D), FFN_HIDDEN ** -0.5),
        "final_norm_w": ones_n((D,)),
    }


def reference(x, c, ada_w, ada_b, norm_mix_w, norm_ffn_w, in_w_even, conv_w, conv_b,
              dt_bias, a_log, d_skip, ssm_norm_w, gmlp_ln_w, gmlp_ln_b, gmlp_ws, gmlp_bs,
              out_w_even, qkv_w, qkv_b, o_w, o_b, sinks, rel_table,
              ffn_gate_w, ffn_up_w, ffn_down_w, final_norm_w):
    cond = jax.nn.silu(c)
    for layer in range(DEPTH):
        mod = cond @ ada_w[layer] + ada_b[layer]
        sh1, sc1, g1, sh2, sc2, g2 = jnp.split(mod, 6, axis=-1)
        h = modulate(rms_norm(x, norm_mix_w[layer]), sh1, sc1)
        i = layer // 2
        if layer % 2 == 0:
            mix = even_mixer(h, in_w_even[i], conv_w[i], conv_b[i], dt_bias[i], a_log[i],
                             d_skip[i], ssm_norm_w[i], gmlp_ln_w[i], gmlp_ln_b[i],
                             gmlp_ws[i], gmlp_bs[i], out_w_even[i])
        else:
            mix = window_attention(h, qkv_w[i], qkv_b[i], o_w[i], o_b[i], sinks[i], rel_table)
        x = x + g1[:, None, :] * mix
        h = modulate(rms_norm(x, norm_ffn_w[layer]), sh2, sc2)
        x = x + g2[:, None, :] * swiglu(h, ffn_gate_w[layer], ffn_up_w[layer], ffn_down_w[layer])
    return rms_norm(x, final_norm_w)
```

```python
import functools
import math

import numpy as np
import jax
import jax.numpy as jnp
from jax import lax
from jax.experimental import pallas as pl
from jax.experimental.pallas import tpu as pltpu

F32 = jnp.float32
BF16 = jnp.bfloat16

EPS = 1e-6
NEG_INF = -1e30

SSM_HEADS = 16
SSM_HEAD_DIM = 64
SSM_GROUPS = 2
SSM_STATE = 128
SSM_CONV = 4
GMLP_GROUPS = 8
ATTN_HEADS = 16
ATTN_KV_HEADS = 2
ATTN_HEAD_DIM = 64
REL_BUCKETS = 32
REL_MAX_DIST = 128
CHUNK = 128

LANES = 128
SUBLANES = 8
VMEM_LIMIT_BYTES = 56 * 1024 * 1024

TOKENS_PER_STEP = 256


def _sigmoid(x):
    return jax.nn.sigmoid(x)


def _silu(x):
    return x * _sigmoid(x)


def _gelu(x):
    return 0.5 * x * (1.0 + lax.erf(x * (1.0 / math.sqrt(2.0))))


def _softplus(x):
    return jnp.maximum(x, 0.0) + jnp.log1p(jnp.exp(-jnp.abs(x)))


def _rms(x):
    return x * lax.rsqrt(jnp.mean(x * x, axis=-1, keepdims=True) + EPS)


def _norm_mod(x, nw, sc, sh):
    return (_rms(x) * nw) * (1.0 + sc) + sh


def _dot(a, b):
    return jnp.dot(a, b, preferred_element_type=F32)


def _dot_nt(a, b):
    return lax.dot_general(a, b, (((1,), (1,)), ((), ())), preferred_element_type=F32)


def _mods_kernel(ct_ref, w_ref, b_ref, o_ref):
    ct = ct_ref[...]
    cond = _silu(ct)
    w = w_ref[0]
    for b in range(ct.shape[1]):
        o_ref[0, b:b + 1, :] = jnp.sum(w * cond[:, b:b + 1], axis=0, keepdims=True) + b_ref[0]


def _mods(c, ada_w, ada_b):
    depth, d, n = ada_w.shape
    bsz = c.shape[0]
    tn = 1536
    return pl.pallas_call(
        _mods_kernel,
        grid=(depth, n // tn),
        in_specs=[
            pl.BlockSpec((d, bsz), lambda l, j: (0, 0)),
            pl.BlockSpec((1, d, tn), lambda l, j: (l, 0, j)),
            pl.BlockSpec((1, 1, tn), lambda l, j: (l, 0, j)),
        ],
        out_specs=pl.BlockSpec((1, bsz, tn), lambda l, j: (l, 0, j)),
        out_shape=jax.ShapeDtypeStruct((depth, bsz, n), F32),
        compiler_params=pltpu.CompilerParams(
            dimension_semantics=("arbitrary", "arbitrary"),
            vmem_limit_bytes=VMEM_LIMIT_BYTES),
        name="adaln_mods",
    )(c.T, ada_w, ada_b.reshape(depth, 1, n))


def _mixer0_kernel(x_ref, sh_ref, sc_ref, g_ref, nw_ref,
                   wz_ref, wxbc_ref, wdt_ref, wu_ref, wv_ref,
                   cw_ref, cb_ref, dtb_ref, alog_ref, dsk_ref, snw_ref,
                   lnw_ref, lnb_ref, ws_ref, bst_ref, wout_ref, e_ref,
                   o_ref,
                   carry_s, state_s, xpad_s, z_s, u_s, v_s, dt_s, ycat_s):
    ts = x_ref.shape[1]
    inner = z_s.shape[1]
    gw = inner // SSM_GROUPS
    L = CHUNK

    @pl.when(pl.program_id(1) == 0)
    def _init():
        carry_s[...] = jnp.zeros_like(carry_s)
        state_s[...] = jnp.zeros_like(state_s)

    x = x_ref[0]
    h = _norm_mod(x, nw_ref[...], sc_ref[0], sh_ref[0]).astype(BF16)
    z_s[...] = _dot(h, wz_ref[...])
    xpad_s[0:SUBLANES, :] = carry_s[...]
    xpad_s[SUBLANES:SUBLANES + ts, :] = _dot(h, wxbc_ref[...])
    carry_s[...] = xpad_s[ts:ts + SUBLANES, :]
    u_s[...] = _dot(h, wu_ref[...])
    v_s[...] = _dot(h, wv_ref[...])
    dt_s[...] = _dot(h, wdt_ref[...])

    row = lax.broadcasted_iota(jnp.int32, (L, L), 0)
    col = lax.broadcasted_iota(jnp.int32, (L, L), 1)
    causal = col <= row
    tril_b = jnp.where(causal, 1.0, 0.0).astype(BF16)
    lane = lax.broadcasted_iota(jnp.int32, (L, LANES), 1)
    lo_half = lane < SSM_HEAD_DIM
    head_lane = lax.broadcasted_iota(jnp.int32, (1, LANES), 1) < SSM_HEADS
    a_neg = jnp.where(head_lane, -jnp.exp(alog_ref[...]), 0.0)
    e_mat = e_ref[...]

    for c in range(ts // L):
        r0 = c * L
        acc = cb_ref[...] + cw_ref[0:1, :] * xpad_s[r0 + 5:r0 + 5 + L, :]
        for k in range(1, SSM_CONV):
            acc = acc + cw_ref[k:k + 1, :] * xpad_s[r0 + 5 + k:r0 + 5 + k + L, :]
        xbc = _silu(acc)
        xs = xbc[:, :inner]
        bm = xbc[:, inner:inner + SSM_GROUPS * SSM_STATE]
        cm = xbc[:, inner + SSM_GROUPS * SSM_STATE:]

        dt = _softplus(dt_s[r0:r0 + L, :] + dtb_ref[...])
        la = dt * a_neg
        la_hi = la.astype(BF16)
        la_lo = (la - la_hi.astype(F32)).astype(BF16)
        cum = _dot(tril_b, la_hi) + _dot(tril_b, la_lo)
        cum_t = cum.T
        cum_last = cum[L - 1:L, :]
        ecum = jnp.exp(cum)
        dte = jnp.exp(cum_last - cum)
        cdec = jnp.broadcast_to(jnp.exp(cum_last), (16, LANES))
        cdec_hi = cdec.astype(BF16)
        cdec_lo = (cdec - cdec_hi.astype(F32)).astype(BF16)
        small = jnp.concatenate(
            [dt.astype(BF16), ecum.astype(BF16), dte.astype(BF16), cdec_hi, cdec_lo], axis=0)
        big = _dot(small, e_mat)
        dt_x = big[0:L]
        ecum_x = big[L:2 * L]
        dte_x = big[2 * L:3 * L]
        cdec_x = big[3 * L:3 * L + 1] + big[3 * L + 16:3 * L + 17]

        xd = xs * dt_x
        xdb = xd.astype(BF16)
        xdw = (xd * dte_x).astype(BF16)

        y_parts = []
        for g in range(SSM_GROUPS):
            bg_t = bm[:, g * SSM_STATE:(g + 1) * SSM_STATE].T.astype(BF16)
            cg = cm[:, g * SSM_STATE:(g + 1) * SSM_STATE].astype(BF16)
            cb = _dot(cg, bg_t)
            prev = state_s[g]
            y_off = _dot(cg, prev.astype(BF16)) * ecum_x[:, g * gw:(g + 1) * gw]
            new = _dot(bg_t, xdw[:, g * gw:(g + 1) * gw])
            state_s[g] = prev * cdec_x[:, g * gw:(g + 1) * gw] + new
            pairs = []
            for p in range(gw // LANES):
                h0 = g * (SSM_HEADS // SSM_GROUPS) + 2 * p
                ms = []
                for hh in (h0, h0 + 1):
                    seg = cum[:, hh:hh + 1] - cum_t[hh:hh + 1, :]
                    dec = jnp.exp(jnp.where(causal, seg, NEG_INF))
                    ms.append((cb * dec).astype(BF16))
                lhs = jnp.concatenate(ms, axis=1)
                xp = xdb[:, h0 * SSM_HEAD_DIM:h0 * SSM_HEAD_DIM + LANES]
                zero = jnp.zeros_like(xp)
                rhs = jnp.concatenate(
                    [jnp.where(lo_half, xp, zero), jnp.where(lo_half, zero, xp)], axis=0)
                pairs.append(_dot(lhs, rhs))
            y_parts.append(jnp.concatenate(pairs, axis=1) + y_off)
        y = jnp.concatenate(y_parts, axis=1) + dsk_ref[...] * xs

        y = y * _silu(z_s[r0:r0 + L, :])
        y = jnp.concatenate([_rms(y[:, g * gw:(g + 1) * gw]) for g in range(SSM_GROUPS)], axis=1)
        ycat_s[r0:r0 + L, 0:inner] = (y * snw_ref[...]).astype(BF16)

        u = _gelu(u_s[r0:r0 + L, :])
        v = _gelu(v_s[r0:r0 + L, :])
        mu = jnp.mean(v, axis=-1, keepdims=True)
        vc = v - mu
        var = jnp.mean(vc * vc, axis=-1, keepdims=True)
        vn = ((vc * lax.rsqrt(var + EPS)) * lnw_ref[...] + lnb_ref[...]).astype(BF16)
        gd = vn.shape[1] // GMLP_GROUPS
        for g in range(GMLP_GROUPS):
            w = jnp.where(causal, ws_ref[g], 0.0).astype(BF16)
            sv = _dot(w, vn[:, g * gd:(g + 1) * gd]) + bst_ref[:, g:g + 1]
            ycat_s[r0:r0 + L, inner + g * gd:inner + (g + 1) * gd] = (
                u[:, g * gd:(g + 1) * gd] * sv).astype(BF16)

    mix = _dot(ycat_s[...], wout_ref[...])
    o_ref[0] = x + g_ref[0] * mix


def _const_spec(shape):
    nd = len(shape)
    return pl.BlockSpec(shape, lambda b, s: (0,) * nd)


def _mixer0(x, sh, sc, g, nw, in_w, conv_w, conv_b, dt_bias, a_log, d_skip, ssm_norm_w,
            ln_w, ln_b, w_s, b_s, out_w):
    bsz, seq, d = x.shape
    ts = TOKENS_PER_STEP
    inner = SSM_HEADS * SSM_HEAD_DIM
    conv_dim = inner + 2 * SSM_GROUPS * SSM_STATE
    ginner = w_s.shape[0] * w_s.shape[1]
    o1 = inner
    o2 = o1 + conv_dim
    o3 = o2 + SSM_HEADS
    o4 = o3 + ginner
    wz = in_w[:, :o1].astype(BF16)
    wxbc = in_w[:, o1:o2].astype(BF16)
    wdt = jnp.pad(in_w[:, o2:o3], ((0, 0), (0, LANES - SSM_HEADS))).astype(BF16)
    wu = in_w[:, o3:o4].astype(BF16)
    wv = in_w[:, o4:].astype(BF16)
    pad_h = lambda a: jnp.pad(a.reshape(1, SSM_HEADS), ((0, 0), (0, LANES - SSM_HEADS)))
    e_np = np.zeros((LANES, inner), np.float32)
    for hh in range(SSM_HEADS):
        e_np[hh, hh * SSM_HEAD_DIM:(hh + 1) * SSM_HEAD_DIM] = 1.0
    consts = [
        nw.reshape(1, d), wz, wxbc, wdt, wu, wv,
        conv_w, conv_b.reshape(1, conv_dim), pad_h(dt_bias), pad_h(a_log),
        jnp.repeat(d_skip, SSM_HEAD_DIM).reshape(1, inner), ssm_norm_w.reshape(1, inner),
        ln_w.reshape(1, ginner), ln_b.reshape(1, ginner), w_s, b_s.T,
        out_w.astype(BF16), jnp.asarray(e_np, BF16),
    ]
    row_spec = pl.BlockSpec((1, 1, d), lambda b, s: (b, 0, 0))
    x_spec = pl.BlockSpec((1, ts, d), lambda b, s: (b, s, 0))
    return pl.pallas_call(
        _mixer0_kernel,
        grid=(bsz, seq // ts),
        in_specs=[x_spec, row_spec, row_spec, row_spec] + [_const_spec(a.shape) for a in consts],
        out_specs=x_spec,
        out_shape=jax.ShapeDtypeStruct(x.shape, F32),
        scratch_shapes=[
            pltpu.VMEM((SUBLANES, conv_dim), F32),
            pltpu.VMEM((SSM_GROUPS, SSM_STATE, inner // SSM_GROUPS), F32),
            pltpu.VMEM((ts + SUBLANES, conv_dim), F32),
            pltpu.VMEM((ts, inner), F32),
            pltpu.VMEM((ts, ginner), F32),
            pltpu.VMEM((ts, ginner), F32),
            pltpu.VMEM((ts, LANES), F32),
            pltpu.VMEM((ts, inner + ginner), BF16),
        ],
        compiler_params=pltpu.CompilerParams(
            dimension_semantics=("arbitrary", "arbitrary"),
            vmem_limit_bytes=VMEM_LIMIT_BYTES),
        name="mixer_ssd_gmlp",
    )(x, sh, sc, g, *consts)


def _folded_buckets():
    dist = (np.arange(CHUNK)[:, None] - np.arange(CHUNK)[None, :]) % CHUNK
    max_exact = REL_BUCKETS // 2
    log_ratio = (np.log(np.maximum(dist, 1).astype(np.float32) / max_exact)
                 / math.log(REL_MAX_DIST / max_exact))
    large = max_exact + (log_ratio * (REL_BUCKETS - max_exact)).astype(np.int32)
    return np.where(dist < max_exact, dist, np.minimum(large, REL_BUCKETS - 1)).astype(np.int32)


def _attn_kernel(sinks_ref, tab_ref, x_ref, sh_ref, sc_ref, g_ref, nw_ref,
                 wqkv_ref, bqkv_ref, wo_ref, bo_ref, bucket_ref,
                 o_ref,
                 bias_s, kprev_s, vprev_s, qkv_s, att_s):
    ts = x_ref.shape[1]
    L = CHUNK
    nq = ATTN_HEADS * ATTN_HEAD_DIM
    kw = ATTN_KV_HEADS * LANES
    first_step = pl.program_id(1) == 0

    @pl.when(jnp.logical_and(pl.program_id(0) == 0, first_step))
    def _build_bias():
        bucket = bucket_ref[...]
        for hh in range(ATTN_HEADS):
            acc = jnp.zeros((L, L), F32)
            for b in range(REL_BUCKETS):
                acc = jnp.where(bucket == b, tab_ref[b * ATTN_HEADS + hh], acc)
            bias_s[hh] = acc

    @pl.when(first_step)
    def _init():
        kprev_s[...] = jnp.zeros_like(kprev_s)
        vprev_s[...] = jnp.zeros_like(vprev_s)

    x = x_ref[0]
    h = _norm_mod(x, nw_ref[...], sc_ref[0], sh_ref[0]).astype(BF16)
    qkv_s[...] = (_dot(h, wqkv_ref[...]) + bqkv_ref[...]).astype(BF16)

    row = lax.broadcasted_iota(jnp.int32, (L, L), 0)
    col = lax.broadcasted_iota(jnp.int32, (L, L), 1)
    upper = col > row
    lo_half = lax.broadcasted_iota(jnp.int32, (L, LANES), 1) < ATTN_HEAD_DIM
    no_prev = jnp.where(first_step, NEG_INF, 0.0)

    for i in range(ts // L):
        r0 = i * L
        for kv in range(ATTN_KV_HEADS):
            k_cur = qkv_s[r0:r0 + L, nq + kv * LANES:nq + (kv + 1) * LANES]
            v_cur = qkv_s[r0:r0 + L, nq + kw + kv * LANES:nq + kw + (kv + 1) * LANES]
            if i == 0:
                k_prev = kprev_s[kv]
                v_prev = vprev_s[kv]
            else:
                k_prev = qkv_s[r0 - L:r0, nq + kv * LANES:nq + (kv + 1) * LANES]
                v_prev = qkv_s[r0 - L:r0, nq + kw + kv * LANES:nq + kw + (kv + 1) * LANES]
            kb = jnp.concatenate([k_prev, k_cur], axis=0)
            vb = jnp.concatenate([v_prev, v_cur], axis=0)
            for p in range(ATTN_HEADS // ATTN_KV_HEADS // 2):
                c0 = (kv * (ATTN_HEADS // ATTN_KV_HEADS // 2) + p) * LANES
                qp = qkv_s[r0:r0 + L, c0:c0 + LANES]
                zero = jnp.zeros_like(qp)
                outs = []
                for half in range(2):
                    hh = c0 // ATTN_HEAD_DIM + half
                    qm = jnp.where(lo_half, qp, zero) if half == 0 else jnp.where(lo_half, zero, qp)
                    l2 = _dot_nt(qm, kb)
                    lg = jnp.where(upper, l2[:, :L], l2[:, L:]) + bias_s[hh]
                    if i == 0:
                        lg = lg + jnp.where(upper, no_prev, 0.0)
                    sink = sinks_ref[hh]
                    m = jnp.maximum(jnp.max(lg, axis=-1, keepdims=True), sink)
                    e = jnp.exp(lg - m)
                    den = jnp.sum(e, axis=-1, keepdims=True) + jnp.exp(sink - m)
                    ez = jnp.zeros_like(e)
                    p2 = jnp.concatenate(
                        [jnp.where(upper, e, ez), jnp.where(upper, ez, e)], axis=1).astype(BF16)
                    outs.append(_dot(p2, vb) / den)
                att_s[r0:r0 + L, c0:c0 + LANES] = jnp.where(lo_half, outs[0], outs[1]).astype(BF16)
    for kv in range(ATTN_KV_HEADS):
        kprev_s[kv] = qkv_s[ts - L:ts, nq + kv * LANES:nq + (kv + 1) * LANES]
        vprev_s[kv] = qkv_s[ts - L:ts, nq + kw + kv * LANES:nq + kw + (kv + 1) * LANES]

    mix = _dot(att_s[...], wo_ref[...]) + bo_ref[...]
    o_ref[0] = x + g_ref[0] * mix


def _attn(x, sh, sc, g, nw, w_qkv, b_qkv, w_o, b_o, sinks, rel_table):
    bsz, seq, d = x.shape
    ts = TOKENS_PER_STEP
    nq = ATTN_HEADS * ATTN_HEAD_DIM
    nkv = ATTN_KV_HEADS * ATTN_HEAD_DIM
    scale = ATTN_HEAD_DIM ** -0.5

    def dup(a):
        lead = a.shape[:-1]
        a = a.reshape(lead + (ATTN_KV_HEADS, 1, ATTN_HEAD_DIM))
        return jnp.broadcast_to(a, lead + (ATTN_KV_HEADS, 2, ATTN_HEAD_DIM)).reshape(lead + (2 * nkv,))

    wq, wk, wv = w_qkv[:, :nq] * scale, w_qkv[:, nq:nq + nkv], w_qkv[:, nq + nkv:]
    bq, bk, bv = b_qkv[:nq] * scale, b_qkv[nq:nq + nkv], b_qkv[nq + nkv:]
    wqkv = jnp.concatenate([wq, dup(wk), dup(wv)], axis=1).astype(BF16)
    bqkv = jnp.concatenate([bq, dup(bk), dup(bv)]).reshape(1, -1)
    width = wqkv.shape[1]
    consts = [nw.reshape(1, d), wqkv, bqkv, w_o.astype(BF16), b_o.reshape(1, d),
              jnp.asarray(_folded_buckets())]
    row_spec = pl.BlockSpec((1, 1, d), lambda b, s: (b, 0, 0))
    x_spec = pl.BlockSpec((1, ts, d), lambda b, s: (b, s, 0))
    smem = pl.BlockSpec(memory_space=pltpu.SMEM)
    return pl.pallas_call(
        _attn_kernel,
        grid=(bsz, seq // ts),
        in_specs=[smem, smem, x_spec, row_spec, row_spec, row_spec]
        + [_const_spec(a.shape) for a in consts],
        out_specs=x_spec,
        out_shape=jax.ShapeDtypeStruct(x.shape, F32),
        scratch_shapes=[
            pltpu.VMEM((ATTN_HEADS, CHUNK, CHUNK), F32),
            pltpu.VMEM((ATTN_KV_HEADS, CHUNK, LANES), BF16),
            pltpu.VMEM((ATTN_KV_HEADS, CHUNK, LANES), BF16),
            pltpu.VMEM((ts, width), BF16),
            pltpu.VMEM((ts, nq), BF16),
        ],
        compiler_params=pltpu.CompilerParams(
            dimension_semantics=("arbitrary", "arbitrary"),
            vmem_limit_bytes=VMEM_LIMIT_BYTES),
        name="mixer_swa",
    )(sinks, rel_table.reshape(-1), x, sh, sc, g, *consts)


def _ffn_kernel(x_ref, sh_ref, sc_ref, g_ref, nw_ref, wg_ref, wu_ref, wd_ref, fnw_ref, o_ref,
                *, final_norm, n_split):
    x = x_ref[0]
    h = _norm_mod(x, nw_ref[...], sc_ref[0], sh_ref[0]).astype(BF16)
    hidden = wg_ref.shape[1]
    step = hidden // n_split
    acc = None
    for j in range(n_split):
        a = _dot(h, wg_ref[:, j * step:(j + 1) * step])
        b = _dot(h, wu_ref[:, j * step:(j + 1) * step])
        part = _dot((_silu(a) * b).astype(BF16), wd_ref[j * step:(j + 1) * step, :])
        acc = part if acc is None else acc + part
    out = x + g_ref[0] * acc
    if final_norm:
        out = _rms(out) * fnw_ref[...]
    o_ref[0] = out


def _ffn(x, sh, sc, g, nw, w_gate, w_up, w_down, final_w, final_norm):
    bsz, seq, d = x.shape
    ts = TOKENS_PER_STEP
    consts = [nw.reshape(1, d), w_gate.astype(BF16), w_up.astype(BF16), w_down.astype(BF16),
              final_w.reshape(1, d)]
    row_spec = pl.BlockSpec((1, 1, d), lambda b, s: (b, 0, 0))
    x_spec = pl.BlockSpec((1, ts, d), lambda b, s: (b, s, 0))
    return pl.pallas_call(
        functools.partial(_ffn_kernel, final_norm=final_norm, n_split=2),
        grid=(bsz, seq // ts),
        in_specs=[x_spec, row_spec, row_spec, row_spec] + [_const_spec(a.shape) for a in consts],
        out_specs=x_spec,
        out_shape=jax.ShapeDtypeStruct(x.shape, F32),
        compiler_params=pltpu.CompilerParams(
            dimension_semantics=("parallel", "parallel"),
            vmem_limit_bytes=VMEM_LIMIT_BYTES),
        name="ffn_swiglu",
    )(x, sh, sc, g, *consts)


def kernel(x, c, ada_w, ada_b, norm_mix_w, norm_ffn_w, in_w_even, conv_w, conv_b, dt_bias, a_log, d_skip, ssm_norm_w, gmlp_ln_w, gmlp_ln_b, gmlp_ws, gmlp_bs, out_w_even, qkv_w, qkv_b, o_w, o_b, sinks, rel_table, ffn_gate_w, ffn_up_w, ffn_down_w, final_norm_w):
    bsz, seq, d = x.shape
    depth = ada_w.shape[0]
    mods = _mods(c, ada_w, ada_b)
    for layer in range(depth):
        m = mods[layer].reshape(bsz, 6, 1, d)
        sh1, sc1, g1, sh2, sc2, g2 = (m[:, j] for j in range(6))
        i = layer // 2
        if layer % 2 == 0:
            x = _mixer0(x, sh1, sc1, g1, norm_mix_w[layer], in_w_even[i], conv_w[i], conv_b[i],
                        dt_bias[i], a_log[i], d_skip[i], ssm_norm_w[i], gmlp_ln_w[i], gmlp_ln_b[i],
                        gmlp_ws[i], gmlp_bs[i], out_w_even[i])
        else:
            x = _attn(x, sh1, sc1, g1, norm_mix_w[layer], qkv_w[i], qkv_b[i], o_w[i], o_b[i],
                      sinks[i], rel_table)
        x = _ffn(x, sh2, sc2, g2, norm_ffn_w[layer], ffn_gate_w[layer], ffn_up_w[layer],
                 ffn_down_w[layer], final_norm_w, final_norm=(layer == depth - 1))
    return x
```

```python
import functools
import math

import numpy as np
import jax
import jax.numpy as jnp
from jax import lax
from jax.experimental import pallas as pl
from jax.experimental.pallas import tpu as pltpu

F32 = jnp.float32
BF16 = jnp.bfloat16

EPS = 1e-6
NEG_INF = -1e30

SSM_HEADS = 16
SSM_HEAD_DIM = 64
SSM_GROUPS = 2
SSM_STATE = 128
SSM_CONV = 4
GMLP_GROUPS = 8
ATTN_HEADS = 16
ATTN_KV_HEADS = 2
ATTN_HEAD_DIM = 64
REL_BUCKETS = 32
REL_MAX_DIST = 128
CHUNK = 128

LANES = 128
SUBLANES = 8
VMEM_LIMIT_BYTES = 56 * 1024 * 1024

MXU_TILE = 256

TOKENS_PER_STEP = 512
FFN_TOKENS_PER_STEP = 512
FFN_HIDDEN_COLS_PER_PASS = 6 * MXU_TILE


def _sigmoid(x):
    return jax.nn.sigmoid(x)


def _silu(x):
    return x * _sigmoid(x)


def _gelu(x):
    return 0.5 * x * (1.0 + lax.erf(x * (1.0 / math.sqrt(2.0))))


def _softplus(x):
    return jnp.maximum(x, 0.0) + jnp.log1p(jnp.exp(-jnp.abs(x)))


def _rms(x):
    return x * lax.rsqrt(jnp.mean(x * x, axis=-1, keepdims=True) + EPS)


def _norm_mod(x, nw, sc, sh):
    return (_rms(x) * nw) * (1.0 + sc) + sh


def _dot(a, b):
    return jnp.dot(a, b, preferred_element_type=F32)


def _dot_nt(a, b):
    return lax.dot_general(a, b, (((1,), (1,)), ((), ())), preferred_element_type=F32)


def _mods_kernel(ct_ref, w_ref, b_ref, o_ref):
    ct = ct_ref[...]
    cond = _silu(ct)
    w = w_ref[0]
    for b in range(ct.shape[1]):
        o_ref[0, b:b + 1, :] = jnp.sum(w * cond[:, b:b + 1], axis=0, keepdims=True) + b_ref[0]


def _mods(c, ada_w, ada_b):
    depth, d, n = ada_w.shape
    bsz = c.shape[0]
    tn = 1536
    return pl.pallas_call(
        _mods_kernel,
        grid=(depth, n // tn),
        in_specs=[
            pl.BlockSpec((d, bsz), lambda l, j: (0, 0)),
            pl.BlockSpec((1, d, tn), lambda l, j: (l, 0, j)),
            pl.BlockSpec((1, 1, tn), lambda l, j: (l, 0, j)),
        ],
        out_specs=pl.BlockSpec((1, bsz, tn), lambda l, j: (l, 0, j)),
        out_shape=jax.ShapeDtypeStruct((depth, bsz, n), F32),
        compiler_params=pltpu.CompilerParams(
            dimension_semantics=("arbitrary", "arbitrary"),
            vmem_limit_bytes=VMEM_LIMIT_BYTES),
        name="adaln_mods",
    )(c.T, ada_w, ada_b.reshape(depth, 1, n))


def _mixer0_kernel(x_ref, sh_ref, sc_ref, g_ref, nw_ref,
                   wz_ref, wxbc_ref, wdt_ref, wu_ref, wv_ref,
                   cw_ref, cb_ref, dtb_ref, alog_ref, dsk_ref, snw_ref,
                   lnw_ref, lnb_ref, ws_ref, bst_ref, wout_ref, e_ref,
                   o_ref,
                   carry_s, state_s, xpad_s, z_s, u_s, v_s, dt_s, ycat_s):
    ts = x_ref.shape[1]
    inner = z_s.shape[1]
    gw = inner // SSM_GROUPS
    L = CHUNK

    @pl.when(pl.program_id(1) == 0)
    def _init():
        carry_s[...] = jnp.zeros_like(carry_s)
        state_s[...] = jnp.zeros_like(state_s)

    x = x_ref[0]
    h = _norm_mod(x, nw_ref[...], sc_ref[0], sh_ref[0]).astype(BF16)
    z_s[...] = _dot(h, wz_ref[...])
    xpad_s[0:SUBLANES, :] = carry_s[...]
    xpad_s[SUBLANES:SUBLANES + ts, :] = _dot(h, wxbc_ref[...])
    carry_s[...] = xpad_s[ts:ts + SUBLANES, :]
    u_s[...] = _dot(h, wu_ref[...])
    v_s[...] = _dot(h, wv_ref[...])
    dt_s[...] = _dot(h, wdt_ref[...])

    row = lax.broadcasted_iota(jnp.int32, (L, L), 0)
    col = lax.broadcasted_iota(jnp.int32, (L, L), 1)
    causal = col <= row
    tril_b = jnp.where(causal, 1.0, 0.0).astype(BF16)
    lane = lax.broadcasted_iota(jnp.int32, (L, LANES), 1)
    lo_half = lane < SSM_HEAD_DIM
    head_lane = lax.broadcasted_iota(jnp.int32, (1, LANES), 1) < SSM_HEADS
    a_neg = jnp.where(head_lane, -jnp.exp(alog_ref[...]), 0.0)
    e_mat = e_ref[...]

    for c in range(ts // L):
        r0 = c * L
        win = xpad_s[r0:r0 + SUBLANES + L, :]
        acc = cw_ref[0:1, :] * win
        for k in range(1, SSM_CONV):
            acc = cw_ref[k:k + 1, :] * win + pltpu.roll(acc, 1, 0)
        xbc = _silu(acc[SUBLANES:, :] + cb_ref[...])
        xs = xbc[:, :inner]
        bm = xbc[:, inner:inner + SSM_GROUPS * SSM_STATE]
        cm = xbc[:, inner + SSM_GROUPS * SSM_STATE:]

        dt = _softplus(dt_s[r0:r0 + L, :] + dtb_ref[...])
        la = dt * a_neg
        la_hi = la.astype(BF16)
        la_lo = (la - la_hi.astype(F32)).astype(BF16)
        cum = _dot(tril_b, la_hi) + _dot(tril_b, la_lo)
        cum_t = cum.T
        cum_last = cum[L - 1:L, :]
        ecum = jnp.exp(cum)
        dte = jnp.exp(cum_last - cum)
        cdec = jnp.broadcast_to(jnp.exp(cum_last), (16, LANES))
        cdec_hi = cdec.astype(BF16)
        cdec_lo = (cdec - cdec_hi.astype(F32)).astype(BF16)
        small = jnp.concatenate(
            [dt.astype(BF16), ecum.astype(BF16), dte.astype(BF16), cdec_hi, cdec_lo], axis=0)
        big = _dot(small, e_mat)
        dt_x = big[0:L]
        ecum_x = big[L:2 * L]
        dte_x = big[2 * L:3 * L]
        cdec_x = big[3 * L:3 * L + 1] + big[3 * L + 16:3 * L + 17]

        xd = xs * dt_x
        xdb = xd.astype(BF16)
        xdw = (xd * dte_x).astype(BF16)

        y_parts = []
        for g in range(SSM_GROUPS):
            bg_t = bm[:, g * SSM_STATE:(g + 1) * SSM_STATE].T.astype(BF16)
            cg = cm[:, g * SSM_STATE:(g + 1) * SSM_STATE].astype(BF16)
            cb = _dot(cg, bg_t)
            prev = state_s[g]
            y_off = _dot(cg, prev.astype(BF16)) * ecum_x[:, g * gw:(g + 1) * gw]
            new = _dot(bg_t, xdw[:, g * gw:(g + 1) * gw])
            state_s[g] = prev * cdec_x[:, g * gw:(g + 1) * gw] + new
            pairs = []
            for p in range(gw // LANES):
                h0 = g * (SSM_HEADS // SSM_GROUPS) + 2 * p
                ms = []
                for hh in (h0, h0 + 1):
                    seg = cum[:, hh:hh + 1] - cum_t[hh:hh + 1, :]
                    dec = jnp.exp(jnp.where(causal, seg, NEG_INF))
                    ms.append((cb * dec).astype(BF16))
                lhs = jnp.concatenate(ms, axis=1)
                xp = xdb[:, h0 * SSM_HEAD_DIM:h0 * SSM_HEAD_DIM + LANES]
                zero = jnp.zeros_like(xp)
                rhs = jnp.concatenate(
                    [jnp.where(lo_half, xp, zero), jnp.where(lo_half, zero, xp)], axis=0)
                pairs.append(_dot(lhs, rhs))
            y_parts.append(jnp.concatenate(pairs, axis=1) + y_off)
        y = jnp.concatenate(y_parts, axis=1) + dsk_ref[...] * xs

        y = y * _silu(z_s[r0:r0 + L, :])
        y = jnp.concatenate([_rms(y[:, g * gw:(g + 1) * gw]) for g in range(SSM_GROUPS)], axis=1)
        ycat_s[r0:r0 + L, 0:inner] = (y * snw_ref[...]).astype(BF16)

        u = _gelu(u_s[r0:r0 + L, :])
        v = _gelu(v_s[r0:r0 + L, :])
        mu = jnp.mean(v, axis=-1, keepdims=True)
        vc = v - mu
        var = jnp.mean(vc * vc, axis=-1, keepdims=True)
        vn = ((vc * lax.rsqrt(var + EPS)) * lnw_ref[...] + lnb_ref[...]).astype(BF16)
        gd = vn.shape[1] // GMLP_GROUPS
        for g in range(GMLP_GROUPS):
            w = jnp.where(causal, ws_ref[g], 0.0).astype(BF16)
            sv = _dot(w, vn[:, g * gd:(g + 1) * gd]) + bst_ref[:, g:g + 1]
            ycat_s[r0:r0 + L, inner + g * gd:inner + (g + 1) * gd] = (
                u[:, g * gd:(g + 1) * gd] * sv).astype(BF16)

    mix = _dot(ycat_s[...], wout_ref[...])
    o_ref[0] = x + g_ref[0] * mix


def _const_spec(shape):
    nd = len(shape)
    return pl.BlockSpec(shape, lambda b, s: (0,) * nd, pipeline_mode=pl.Buffered(1))


def _mixer0(x, sh, sc, g, nw, in_w, conv_w, conv_b, dt_bias, a_log, d_skip, ssm_norm_w,
            ln_w, ln_b, w_s, b_s, out_w):
    bsz, seq, d = x.shape
    ts = TOKENS_PER_STEP
    inner = SSM_HEADS * SSM_HEAD_DIM
    conv_dim = inner + 2 * SSM_GROUPS * SSM_STATE
    ginner = w_s.shape[0] * w_s.shape[1]
    o1 = inner
    o2 = o1 + conv_dim
    o3 = o2 + SSM_HEADS
    o4 = o3 + ginner
    wz = in_w[:, :o1].astype(BF16)
    wxbc = in_w[:, o1:o2].astype(BF16)
    wdt = jnp.pad(in_w[:, o2:o3], ((0, 0), (0, LANES - SSM_HEADS))).astype(BF16)
    wu = in_w[:, o3:o4].astype(BF16)
    wv = in_w[:, o4:].astype(BF16)
    pad_h = lambda a: jnp.pad(a.reshape(1, SSM_HEADS), ((0, 0), (0, LANES - SSM_HEADS)))
    e_np = np.zeros((LANES, inner), np.float32)
    for hh in range(SSM_HEADS):
        e_np[hh, hh * SSM_HEAD_DIM:(hh + 1) * SSM_HEAD_DIM] = 1.0
    consts = [
        nw.reshape(1, d), wz, wxbc, wdt, wu, wv,
        conv_w, conv_b.reshape(1, conv_dim), pad_h(dt_bias), pad_h(a_log),
        jnp.repeat(d_skip, SSM_HEAD_DIM).reshape(1, inner), ssm_norm_w.reshape(1, inner),
        ln_w.reshape(1, ginner), ln_b.reshape(1, ginner), w_s, b_s.T,
        out_w.astype(BF16), jnp.asarray(e_np, BF16),
    ]
    row_spec = pl.BlockSpec((1, 1, d), lambda b, s: (b, 0, 0))
    x_spec = pl.BlockSpec((1, ts, d), lambda b, s: (b, s, 0))
    return pl.pallas_call(
        _mixer0_kernel,
        grid=(bsz, seq // ts),
        in_specs=[x_spec, row_spec, row_spec, row_spec] + [_const_spec(a.shape) for a in consts],
        out_specs=x_spec,
        out_shape=jax.ShapeDtypeStruct(x.shape, F32),
        scratch_shapes=[
            pltpu.VMEM((SUBLANES, conv_dim), F32),
            pltpu.VMEM((SSM_GROUPS, SSM_STATE, inner // SSM_GROUPS), F32),
            pltpu.VMEM((ts + SUBLANES, conv_dim), F32),
            pltpu.VMEM((ts, inner), F32),
            pltpu.VMEM((ts, ginner), F32),
            pltpu.VMEM((ts, ginner), F32),
            pltpu.VMEM((ts, LANES), F32),
            pltpu.VMEM((ts, inner + ginner), BF16),
        ],
        compiler_params=pltpu.CompilerParams(
            dimension_semantics=("arbitrary", "arbitrary"),
            vmem_limit_bytes=VMEM_LIMIT_BYTES),
        name="mixer_ssd_gmlp",
    )(x, sh, sc, g, *consts)


def _folded_buckets():
    dist = (np.arange(CHUNK)[:, None] - np.arange(CHUNK)[None, :]) % CHUNK
    max_exact = REL_BUCKETS // 2
    log_ratio = (np.log(np.maximum(dist, 1).astype(np.float32) / max_exact)
                 / math.log(REL_MAX_DIST / max_exact))
    large = max_exact + (log_ratio * (REL_BUCKETS - max_exact)).astype(np.int32)
    return np.where(dist < max_exact, dist, np.minimum(large, REL_BUCKETS - 1)).astype(np.int32)


def _attn_kernel(sinks_ref, tab_ref, x_ref, sh_ref, sc_ref, g_ref, nw_ref,
                 wqkv_ref, bqkv_ref, wo_ref, bo_ref, bucket_ref,
                 o_ref,
                 bias_s, kprev_s, vprev_s, qkv_s, att_s):
    ts = x_ref.shape[1]
    L = CHUNK
    nq = ATTN_HEADS * ATTN_HEAD_DIM
    kw = ATTN_KV_HEADS * LANES
    first_step = pl.program_id(1) == 0

    @pl.when(jnp.logical_and(pl.program_id(0) == 0, first_step))
    def _build_bias():
        bucket = bucket_ref[...]
        for hh in range(ATTN_HEADS):
            acc = jnp.zeros((L, L), F32)
            for b in range(REL_BUCKETS):
                acc = jnp.where(bucket == b, tab_ref[b * ATTN_HEADS + hh], acc)
            bias_s[hh] = acc

    @pl.when(first_step)
    def _init():
        kprev_s[...] = jnp.zeros_like(kprev_s)
        vprev_s[...] = jnp.zeros_like(vprev_s)

    x = x_ref[0]
    h = _norm_mod(x, nw_ref[...], sc_ref[0], sh_ref[0]).astype(BF16)
    qkv_s[...] = (_dot(h, wqkv_ref[...]) + bqkv_ref[...]).astype(BF16)

    row = lax.broadcasted_iota(jnp.int32, (L, L), 0)
    col = lax.broadcasted_iota(jnp.int32, (L, L), 1)
    upper = col > row
    lo_half = lax.broadcasted_iota(jnp.int32, (L, LANES), 1) < ATTN_HEAD_DIM
    no_prev = jnp.where(first_step, NEG_INF, 0.0)

    for i in range(ts // L):
        r0 = i * L
        for kv in range(ATTN_KV_HEADS):
            k_cur = qkv_s[r0:r0 + L, nq + kv * LANES:nq + (kv + 1) * LANES]
            v_cur = qkv_s[r0:r0 + L, nq + kw + kv * LANES:nq + kw + (kv + 1) * LANES]
            if i == 0:
                k_prev = kprev_s[kv]
                v_prev = vprev_s[kv]
            else:
                k_prev = qkv_s[r0 - L:r0, nq + kv * LANES:nq + (kv + 1) * LANES]
                v_prev = qkv_s[r0 - L:r0, nq + kw + kv * LANES:nq + kw + (kv + 1) * LANES]
            kb = jnp.concatenate([k_prev, k_cur], axis=0)
            vb = jnp.concatenate([v_prev, v_cur], axis=0)
            for p in range(ATTN_HEADS // ATTN_KV_HEADS // 2):
                c0 = (kv * (ATTN_HEADS // ATTN_KV_HEADS // 2) + p) * LANES
                qp = qkv_s[r0:r0 + L, c0:c0 + LANES]
                zero = jnp.zeros_like(qp)
                outs = []
                for half in range(2):
                    hh = c0 // ATTN_HEAD_DIM + half
                    qm = jnp.where(lo_half, qp, zero) if half == 0 else jnp.where(lo_half, zero, qp)
                    l2 = _dot_nt(qm, kb)
                    lg = jnp.where(upper, l2[:, :L], l2[:, L:]) + bias_s[hh]
                    if i == 0:
                        lg = lg + jnp.where(upper, no_prev, 0.0)
                    sink = sinks_ref[hh]
                    m = jnp.maximum(jnp.max(lg, axis=-1, keepdims=True), sink)
                    e = jnp.exp(lg - m)
                    den = jnp.sum(e, axis=-1, keepdims=True) + jnp.exp(sink - m)
                    ez = jnp.zeros_like(e)
                    p2 = jnp.concatenate(
                        [jnp.where(upper, e, ez), jnp.where(upper, ez, e)], axis=1).astype(BF16)
                    outs.append(_dot(p2, vb) / den)
                att_s[r0:r0 + L, c0:c0 + LANES] = jnp.where(lo_half, outs[0], outs[1]).astype(BF16)
    for kv in range(ATTN_KV_HEADS):
        kprev_s[kv] = qkv_s[ts - L:ts, nq + kv * LANES:nq + (kv + 1) * LANES]
        vprev_s[kv] = qkv_s[ts - L:ts, nq + kw + kv * LANES:nq + kw + (kv + 1) * LANES]

    mix = _dot(att_s[...], wo_ref[...]) + bo_ref[...]
    o_ref[0] = x + g_ref[0] * mix


def _attn(x, sh, sc, g, nw, w_qkv, b_qkv, w_o, b_o, sinks, rel_table):
    bsz, seq, d = x.shape
    ts = TOKENS_PER_STEP
    nq = ATTN_HEADS * ATTN_HEAD_DIM
    nkv = ATTN_KV_HEADS * ATTN_HEAD_DIM
    scale = ATTN_HEAD_DIM ** -0.5

    def dup(a):
        lead = a.shape[:-1]
        a = a.reshape(lead + (ATTN_KV_HEADS, 1, ATTN_HEAD_DIM))
        return jnp.broadcast_to(a, lead + (ATTN_KV_HEADS, 2, ATTN_HEAD_DIM)).reshape(lead + (2 * nkv,))

    wq, wk, wv = w_qkv[:, :nq] * scale, w_qkv[:, nq:nq + nkv], w_qkv[:, nq + nkv:]
    bq, bk, bv = b_qkv[:nq] * scale, b_qkv[nq:nq + nkv], b_qkv[nq + nkv:]
    wqkv = jnp.concatenate([wq, dup(wk), dup(wv)], axis=1).astype(BF16)
    bqkv = jnp.concatenate([bq, dup(bk), dup(bv)]).reshape(1, -1)
    width = wqkv.shape[1]
    consts = [nw.reshape(1, d), wqkv, bqkv, w_o.astype(BF16), b_o.reshape(1, d),
              jnp.asarray(_folded_buckets())]
    row_spec = pl.BlockSpec((1, 1, d), lambda b, s: (b, 0, 0))
    x_spec = pl.BlockSpec((1, ts, d), lambda b, s: (b, s, 0))
    smem = pl.BlockSpec(memory_space=pltpu.SMEM)
    return pl.pallas_call(
        _attn_kernel,
        grid=(bsz, seq // ts),
        in_specs=[smem, smem, x_spec, row_spec, row_spec, row_spec]
        + [_const_spec(a.shape) for a in consts],
        out_specs=x_spec,
        out_shape=jax.ShapeDtypeStruct(x.shape, F32),
        scratch_shapes=[
            pltpu.VMEM((ATTN_HEADS, CHUNK, CHUNK), F32),
            pltpu.VMEM((ATTN_KV_HEADS, CHUNK, LANES), BF16),
            pltpu.VMEM((ATTN_KV_HEADS, CHUNK, LANES), BF16),
            pltpu.VMEM((ts, width), BF16),
            pltpu.VMEM((ts, nq), BF16),
        ],
        compiler_params=pltpu.CompilerParams(
            dimension_semantics=("arbitrary", "arbitrary"),
            vmem_limit_bytes=VMEM_LIMIT_BYTES),
        name="mixer_swa",
    )(sinks, rel_table.reshape(-1), x, sh, sc, g, *consts)


def _hidden_splits(hidden, max_cols):
    bounds, lo = [], 0
    while lo < hidden:
        hi = min(hidden, lo + max_cols)
        bounds.append((lo, hi))
        lo = hi
    return bounds


def _ffn_kernel(x_ref, sh_ref, sc_ref, g_ref, nw_ref, wg_ref, wu_ref, wd_ref, fnw_ref, o_ref,
                *, final_norm, splits):
    x = x_ref[0]
    h = _norm_mod(x, nw_ref[...], sc_ref[0], sh_ref[0]).astype(BF16)
    acc = None
    for lo, hi in splits:
        a = _dot(h, wg_ref[:, lo:hi])
        b = _dot(h, wu_ref[:, lo:hi])
        part = _dot((_silu(a) * b).astype(BF16), wd_ref[lo:hi, :])
        acc = part if acc is None else acc + part
    out = x + g_ref[0] * acc
    if final_norm:
        out = _rms(out) * fnw_ref[...]
    o_ref[0] = out


def _ffn(x, sh, sc, g, nw, w_gate, w_up, w_down, final_w, final_norm):
    bsz, seq, d = x.shape
    ts = FFN_TOKENS_PER_STEP
    splits = _hidden_splits(w_gate.shape[1], FFN_HIDDEN_COLS_PER_PASS)
    consts = [nw.reshape(1, d), w_gate.astype(BF16), w_up.astype(BF16), w_down.astype(BF16),
              final_w.reshape(1, d)]
    row_spec = pl.BlockSpec((1, 1, d), lambda b, s: (b, 0, 0))
    x_spec = pl.BlockSpec((1, ts, d), lambda b, s: (b, s, 0))
    return pl.pallas_call(
        functools.partial(_ffn_kernel, final_norm=final_norm, splits=splits),
        grid=(bsz, seq // ts),
        in_specs=[x_spec, row_spec, row_spec, row_spec] + [_const_spec(a.shape) for a in consts],
        out_specs=x_spec,
        out_shape=jax.ShapeDtypeStruct(x.shape, F32),
        compiler_params=pltpu.CompilerParams(
            dimension_semantics=("parallel", "parallel"),
            vmem_limit_bytes=VMEM_LIMIT_BYTES),
        name="ffn_swiglu",
    )(x, sh, sc, g, *consts)


def kernel(x, c, ada_w, ada_b, norm_mix_w, norm_ffn_w, in_w_even, conv_w, conv_b, dt_bias, a_log, d_skip, ssm_norm_w, gmlp_ln_w, gmlp_ln_b, gmlp_ws, gmlp_bs, out_w_even, qkv_w, qkv_b, o_w, o_b, sinks, rel_table, ffn_gate_w, ffn_up_w, ffn_down_w, final_norm_w):
    bsz, seq, d = x.shape
    depth = ada_w.shape[0]
    mods = _mods(c, ada_w, ada_b)
    for layer in range(depth):
        m = mods[layer].reshape(bsz, 6, 1, d)
        sh1, sc1, g1, sh2, sc2, g2 = (m[:, j] for j in range(6))
        i = layer // 2
        if layer % 2 == 0:
            x = _mixer0(x, sh1, sc1, g1, norm_mix_w[layer], in_w_even[i], conv_w[i], conv_b[i],
                        dt_bias[i], a_log[i], d_skip[i], ssm_norm_w[i], gmlp_ln_w[i], gmlp_ln_b[i],
                        gmlp_ws[i], gmlp_bs[i], out_w_even[i])
        else:
            x = _attn(x, sh1, sc1, g1, norm_mix_w[layer], qkv_w[i], qkv_b[i], o_w[i], o_b[i],
                      sinks[i], rel_table)
        x = _ffn(x, sh2, sc2, g2, norm_ffn_w[layer], ffn_gate_w[layer], ffn_up_w[layer],
                 ffn_down_w[layer], final_norm_w, final_norm=(layer == depth - 1))
    return x
```

```python
import functools
import math

import numpy as np
import jax
import jax.numpy as jnp
from jax import lax
from jax.experimental import pallas as pl
from jax.experimental.pallas import tpu as pltpu

F32 = jnp.float32
BF16 = jnp.bfloat16

EPS = 1e-6
NEG_INF = -1e30

SSM_HEADS = 16
SSM_HEAD_DIM = 64
SSM_GROUPS = 2
SSM_STATE = 128
SSM_CONV = 4
GMLP_GROUPS = 8
ATTN_HEADS = 16
ATTN_KV_HEADS = 2
ATTN_HEAD_DIM = 64
REL_BUCKETS = 32
REL_MAX_DIST = 128
CHUNK = 128

LANES = 128
SUBLANES = 8
VMEM_LIMIT_BYTES = 56 * 1024 * 1024

MXU_TILE = 256

TOKENS_PER_STEP = 512
SSD_TOKENS_PER_STEP = 1024
MIXER_ROWS_PER_SUBBLOCK = 256
MIXER_SLOTS = 2
MIXER_COLS_PER_TASK = MXU_TILE
FFN_TOKENS_PER_STEP = 512
FFN_HIDDEN_COLS_PER_PASS = 6 * MXU_TILE
FFN_ROW_SUBBLOCKS = 2


def _sigmoid(x):
    return jax.nn.sigmoid(x)


def _silu(x):
    return x * _sigmoid(x)


def _gelu(x):
    return 0.5 * x * (1.0 + lax.erf(x * (1.0 / math.sqrt(2.0))))


def _softplus(x):
    return jnp.maximum(x, 0.0) + jnp.log(1.0 + jnp.exp(-jnp.abs(x)))


def _rms(x):
    return x * lax.rsqrt(jnp.mean(x * x, axis=-1, keepdims=True) + EPS)


def _norm_mod(x, nw, sc, sh):
    return (_rms(x) * nw) * (1.0 + sc) + sh


def _dot(a, b):
    return jnp.dot(a, b, preferred_element_type=F32)


def _dot_nt(a, b):
    return lax.dot_general(a, b, (((1,), (1,)), ((), ())), preferred_element_type=F32)


def _mods_kernel(ct_ref, w_ref, b_ref, o_ref):
    ct = ct_ref[...]
    cond = _silu(ct)
    w = w_ref[0]
    for b in range(ct.shape[1]):
        o_ref[0, b:b + 1, :] = jnp.sum(w * cond[:, b:b + 1], axis=0, keepdims=True) + b_ref[0]


def _mods(c, ada_w, ada_b):
    depth, d, n = ada_w.shape
    bsz = c.shape[0]
    tn = 1536
    return pl.pallas_call(
        _mods_kernel,
        grid=(depth, n // tn),
        in_specs=[
            pl.BlockSpec((d, bsz), lambda l, j: (0, 0)),
            pl.BlockSpec((1, d, tn), lambda l, j: (l, 0, j)),
            pl.BlockSpec((1, 1, tn), lambda l, j: (l, 0, j)),
        ],
        out_specs=pl.BlockSpec((1, bsz, tn), lambda l, j: (l, 0, j)),
        out_shape=jax.ShapeDtypeStruct((depth, bsz, n), F32),
        compiler_params=pltpu.CompilerParams(
            dimension_semantics=("arbitrary", "arbitrary"),
            vmem_limit_bytes=VMEM_LIMIT_BYTES),
        name="adaln_mods",
    )(c.T, ada_w, ada_b.reshape(depth, 1, n))


def _mixer0_kernel(x_ref, sh_ref, sc_ref, g_ref, nw_ref,
                   win_ref,
                   cw_ref, cb_ref, dtb_ref, alog_ref, dsk_ref, snw_ref,
                   lnw_ref, lnb_ref, ws_ref, bst_ref, wout_ref, e_ref,
                   o_ref,
                   carry_s, state_s, *slot_scratch):
    ts = x_ref.shape[1]
    L = CHUNK
    per_slot = len(slot_scratch) // MIXER_SLOTS
    slots = [slot_scratch[i * per_slot:(i + 1) * per_slot] for i in range(MIXER_SLOTS)]
    rb, d = slots[0][0].shape
    inner = slots[0][3].shape[1]
    ginner = slots[0][4].shape[1]
    gw = inner // SSM_GROUPS
    n_slab = slots[0][1].shape[0]
    n_xs = inner // LANES
    c_u = inner + n_slab * LANES
    c_v = c_u + ginner
    c_dt = c_v + ginner
    n_sub = ts // rb
    gd = ginner // GMLP_GROUPS

    @pl.when(pl.program_id(1) == 0)
    def _init():
        carry_s[...] = jnp.zeros_like(carry_s)
        state_s[...] = jnp.zeros_like(state_s)

    row = lax.broadcasted_iota(jnp.int32, (L, L), 0)
    col = lax.broadcasted_iota(jnp.int32, (L, L), 1)
    causal = col <= row
    tril_b = jnp.where(causal, 1.0, 0.0).astype(BF16)
    lo_half = lax.broadcasted_iota(jnp.int32, (L, LANES), 1) < SSM_HEAD_DIM
    head_lane = lax.broadcasted_iota(jnp.int32, (1, LANES), 1) < SSM_HEADS
    a_neg = jnp.where(head_lane, -jnp.exp(alog_ref[...]), 0.0)
    ws_pairs = [
        jnp.concatenate([jnp.where(causal, ws_ref[2 * gp], 0.0),
                         jnp.where(causal, ws_ref[2 * gp + 1], 0.0)], axis=1).astype(BF16)
        for gp in range(GMLP_GROUPS // 2)]

    pending = []

    def fill():
        if pending:
            pending.pop(0)()

    def project_tasks(k):
        hb_s, xpad_s, _, z_s, u_s, v_s, dt_s, _ = slots[k % MIXER_SLOTS]
        rs = slice(k * rb, (k + 1) * rb)

        def head():
            hb_s[...] = _norm_mod(x_ref[0, rs, :], nw_ref[...], sc_ref[0], sh_ref[0]).astype(BF16)
            for j in range(n_slab):
                if k == 0:
                    xpad_s[j, 0:SUBLANES, :] = carry_s[j]
                else:
                    prev_xpad = slots[(k - 1) % MIXER_SLOTS][1]
                    xpad_s[j, 0:SUBLANES, :] = prev_xpad[j, rb:rb + SUBLANES, :]

        def to_cols(dst, c_src, c0, c1):
            def run():
                dst[:, c0:c1] = _dot(hb_s[...], win_ref[:, c_src + c0:c_src + c1])
            return run

        def to_slabs(j0, j1):
            def run():
                r = _dot(hb_s[...], win_ref[:, inner + j0 * LANES:inner + j1 * LANES])
                for j in range(j0, j1):
                    xpad_s[j, SUBLANES:SUBLANES + rb, :] = r[:, (j - j0) * LANES:(j - j0 + 1) * LANES]
            return run

        step = MIXER_COLS_PER_TASK
        tasks = [head]
        tasks += [to_cols(z_s, 0, c0, c0 + step) for c0 in range(0, inner, step)]
        tasks += [to_slabs(j0, j0 + step // LANES) for j0 in range(0, n_slab, step // LANES)]
        tasks += [to_cols(u_s, c_u, c0, c0 + step) for c0 in range(0, ginner, step)]
        tasks += [to_cols(v_s, c_v, c0, c0 + step) for c0 in range(0, ginner, step)]
        tasks.append(to_cols(dt_s, c_dt, 0, LANES))
        return tasks

    def out_tasks(k):
        ycat_s = slots[k % MIXER_SLOTS][7]
        rs = slice(k * rb, (k + 1) * rb)

        def cols(c0, c1):
            def run():
                mix = _dot(ycat_s[...], wout_ref[:, c0:c1])
                o_ref[0, rs, c0:c1] = x_ref[0, rs, c0:c1] + g_ref[0][:, c0:c1] * mix
            return run

        return [cols(c0, c0 + MXU_TILE) for c0 in range(0, d, MXU_TILE)]

    def chunk_work(k, ci):
        _, xpad_s, xact_s, z_s, u_s, v_s, dt_s, ycat_s = slots[k % MIXER_SLOTS]
        r0 = ci * L
        for j in range(n_slab):
            cw = cw_ref[:, j * LANES:(j + 1) * LANES]
            base = r0 + SUBLANES - (SSM_CONV - 1)
            acc = cb_ref[:, j * LANES:(j + 1) * LANES] + cw[0:1] * xpad_s[j, base:base + L, :]
            for t in range(1, SSM_CONV):
                acc = acc + cw[t:t + 1] * xpad_s[j, base + t:base + t + L, :]
            xact_s[j, r0:r0 + L, :] = _silu(acc)
        fill()
        xs = jnp.concatenate([xact_s[j, r0:r0 + L, :] for j in range(n_xs)], axis=1)

        dt = _softplus(dt_s[r0:r0 + L, :] + dtb_ref[...])
        la = dt * a_neg
        la_hi = la.astype(BF16)
        la_lo = (la - la_hi.astype(F32)).astype(BF16)
        cum2 = _dot(tril_b, jnp.concatenate([la_hi, la_lo], axis=1))
        cum = cum2[:, :LANES] + cum2[:, LANES:]
        fill()
        cum_t = cum.T
        dt_t = dt.T
        cum_last = cum[L - 1:L, :]
        cdec = jnp.broadcast_to(jnp.exp(cum_last), (16, LANES))
        cdec_hi = cdec.astype(BF16)
        cdec_lo = (cdec - cdec_hi.astype(F32)).astype(BF16)
        small = jnp.concatenate(
            [jnp.exp(cum).astype(BF16),
             (dt * jnp.exp(cum_last - cum)).astype(BF16),
             cdec_hi, cdec_lo], axis=0)
        spread = _dot(small, e_ref[...])
        ecum_x = spread[0:L]
        w_x = spread[L:2 * L]
        cdec_x = spread[2 * L:2 * L + 1] + spread[2 * L + 16:2 * L + 17]
        fill()
        xsb = xs.astype(BF16)
        xdw = (xs * w_x).astype(BF16)

        y_parts = []
        for g in range(SSM_GROUPS):
            bg_t = xact_s[n_xs + g, r0:r0 + L, :].T.astype(BF16)
            cg = xact_s[n_xs + SSM_GROUPS + g, r0:r0 + L, :].astype(BF16)
            cb = _dot(cg, bg_t)
            prev = state_s[g]
            y_off = _dot(cg, prev.astype(BF16)) * ecum_x[:, g * gw:(g + 1) * gw]
            new = _dot(bg_t, xdw[:, g * gw:(g + 1) * gw])
            state_s[g] = prev * cdec_x[:, g * gw:(g + 1) * gw] + new
            fill()
            pairs = []
            for p in range(gw // LANES):
                h0 = g * (SSM_HEADS // SSM_GROUPS) + 2 * p
                ms = []
                for hh in (h0, h0 + 1):
                    seg = cum[:, hh:hh + 1] - cum_t[hh:hh + 1, :]
                    dec = jnp.exp(jnp.where(causal, seg, NEG_INF))
                    ms.append((cb * dec * dt_t[hh:hh + 1, :]).astype(BF16))
                lhs = jnp.concatenate(ms, axis=1)
                xp = xsb[:, h0 * SSM_HEAD_DIM:h0 * SSM_HEAD_DIM + LANES]
                zero = jnp.zeros_like(xp)
                rhs = jnp.concatenate(
                    [jnp.where(lo_half, xp, zero), jnp.where(lo_half, zero, xp)], axis=0)
                pairs.append(_dot(lhs, rhs))
                if p % 2 == 1:
                    fill()
            y_parts.append(jnp.concatenate(pairs, axis=1) + y_off)
        y = jnp.concatenate(y_parts, axis=1) + dsk_ref[...] * xs

        y = y * _silu(z_s[r0:r0 + L, :])
        y = jnp.concatenate([_rms(y[:, g * gw:(g + 1) * gw]) for g in range(SSM_GROUPS)], axis=1)
        ycat_s[r0:r0 + L, 0:inner] = (y * snw_ref[...]).astype(BF16)
        fill()

        u = _gelu(u_s[r0:r0 + L, :])
        v = _gelu(v_s[r0:r0 + L, :])
        mu = jnp.mean(v, axis=-1, keepdims=True)
        vc = v - mu
        var = jnp.mean(vc * vc, axis=-1, keepdims=True)
        vn = ((vc * lax.rsqrt(var + EPS)) * lnw_ref[...] + lnb_ref[...]).astype(BF16)
        for gp in range(GMLP_GROUPS // 2):
            va = vn[:, 2 * gp * gd:(2 * gp + 1) * gd]
            vb = vn[:, (2 * gp + 1) * gd:(2 * gp + 2) * gd]
            vz = jnp.zeros_like(va)
            rhs = jnp.concatenate([jnp.concatenate([va, vz], axis=1),
                                   jnp.concatenate([vz, vb], axis=1)], axis=0)
            sv2 = _dot(ws_pairs[gp], rhs)
            for half in range(2):
                g = 2 * gp + half
                sv = sv2[:, half * gd:(half + 1) * gd] + bst_ref[:, g:g + 1]
                ycat_s[r0:r0 + L, inner + g * gd:inner + (g + 1) * gd] = (
                    u[:, g * gd:(g + 1) * gd] * sv).astype(BF16)
            if gp % 2 == 1:
                fill()

    for task in project_tasks(0):
        task()
    for k in range(n_sub):
        if k >= 1:
            pending.extend(out_tasks(k - 1))
        if k + 1 < n_sub:
            pending.extend(project_tasks(k + 1))
        for ci in range(rb // L):
            chunk_work(k, ci)
        while pending:
            fill()
    for task in out_tasks(n_sub - 1):
        task()
    last_xpad = slots[(n_sub - 1) % MIXER_SLOTS][1]
    for j in range(n_slab):
        carry_s[j] = last_xpad[j, rb:rb + SUBLANES, :]


def _const_spec(shape):
    nd = len(shape)
    return pl.BlockSpec(shape, lambda b, s: (0,) * nd, pipeline_mode=pl.Buffered(1))


def _mixer0(x, sh, sc, g, nw, in_w, conv_w, conv_b, dt_bias, a_log, d_skip, ssm_norm_w,
            ln_w, ln_b, w_s, b_s, out_w):
    bsz, seq, d = x.shape
    ts = SSD_TOKENS_PER_STEP
    rb = MIXER_ROWS_PER_SUBBLOCK
    inner = SSM_HEADS * SSM_HEAD_DIM
    conv_dim = inner + 2 * SSM_GROUPS * SSM_STATE
    ginner = w_s.shape[0] * w_s.shape[1]
    o1 = inner
    o2 = o1 + conv_dim
    o3 = o2 + SSM_HEADS
    w_in = jnp.concatenate(
        [in_w[:, :o2], in_w[:, o3:],
         jnp.pad(in_w[:, o2:o3], ((0, 0), (0, LANES - SSM_HEADS)))], axis=1).astype(BF16)
    pad_h = lambda a: jnp.pad(a.reshape(1, SSM_HEADS), ((0, 0), (0, LANES - SSM_HEADS)))
    e_np = np.zeros((LANES, inner), np.float32)
    for hh in range(SSM_HEADS):
        e_np[hh, hh * SSM_HEAD_DIM:(hh + 1) * SSM_HEAD_DIM] = 1.0
    consts = [
        nw.reshape(1, d), w_in,
        conv_w, conv_b.reshape(1, conv_dim), pad_h(dt_bias), pad_h(a_log),
        jnp.repeat(d_skip, SSM_HEAD_DIM).reshape(1, inner), ssm_norm_w.reshape(1, inner),
        ln_w.reshape(1, ginner), ln_b.reshape(1, ginner), w_s, b_s.T,
        out_w.astype(BF16), jnp.asarray(e_np, BF16),
    ]
    slot_scratch = [
        pltpu.VMEM((rb, d), BF16),
        pltpu.VMEM((conv_dim // LANES, rb + SUBLANES, LANES), F32),
        pltpu.VMEM((conv_dim // LANES, rb, LANES), F32),
        pltpu.VMEM((rb, inner), F32),
        pltpu.VMEM((rb, ginner), F32),
        pltpu.VMEM((rb, ginner), F32),
        pltpu.VMEM((rb, LANES), F32),
        pltpu.VMEM((rb, inner + ginner), BF16),
    ]
    row_spec = pl.BlockSpec((1, 1, d), lambda b, s: (b, 0, 0))
    x_spec = pl.BlockSpec((1, ts, d), lambda b, s: (b, s, 0))
    return pl.pallas_call(
        _mixer0_kernel,
        grid=(bsz, seq // ts),
        in_specs=[x_spec, row_spec, row_spec, row_spec] + [_const_spec(a.shape) for a in consts],
        out_specs=x_spec,
        out_shape=jax.ShapeDtypeStruct(x.shape, F32),
        scratch_shapes=[
            pltpu.VMEM((conv_dim // LANES, SUBLANES, LANES), F32),
            pltpu.VMEM((SSM_GROUPS, SSM_STATE, inner // SSM_GROUPS), F32),
        ] + slot_scratch * MIXER_SLOTS,
        compiler_params=pltpu.CompilerParams(
            dimension_semantics=("arbitrary", "arbitrary"),
            vmem_limit_bytes=VMEM_LIMIT_BYTES),
        name="mixer_ssd_gmlp",
    )(x, sh, sc, g, *consts)


def _folded_buckets():
    dist = (np.arange(CHUNK)[:, None] - np.arange(CHUNK)[None, :]) % CHUNK
    max_exact = REL_BUCKETS // 2
    log_ratio = (np.log(np.maximum(dist, 1).astype(np.float32) / max_exact)
                 / math.log(REL_MAX_DIST / max_exact))
    large = max_exact + (log_ratio * (REL_BUCKETS - max_exact)).astype(np.int32)
    return np.where(dist < max_exact, dist, np.minimum(large, REL_BUCKETS - 1)).astype(np.int32)


def _attn_kernel(sinks_ref, tab_ref, x_ref, sh_ref, sc_ref, g_ref, nw_ref,
                 wqkv_ref, bqkv_ref, wo_ref, bo_ref, bucket_ref,
                 o_ref,
                 bias_s, kprev_s, vprev_s, qkv_s, att_s):
    ts = x_ref.shape[1]
    L = CHUNK
    nq = ATTN_HEADS * ATTN_HEAD_DIM
    kw = ATTN_KV_HEADS * LANES
    first_step = pl.program_id(1) == 0

    @pl.when(jnp.logical_and(pl.program_id(0) == 0, first_step))
    def _build_bias():
        bucket = bucket_ref[...]
        for hh in range(ATTN_HEADS):
            acc = jnp.zeros((L, L), F32)
            for b in range(REL_BUCKETS):
                acc = jnp.where(bucket == b, tab_ref[b * ATTN_HEADS + hh], acc)
            bias_s[hh] = acc

    @pl.when(first_step)
    def _init():
        kprev_s[...] = jnp.zeros_like(kprev_s)
        vprev_s[...] = jnp.zeros_like(vprev_s)

    x = x_ref[0]
    h = _norm_mod(x, nw_ref[...], sc_ref[0], sh_ref[0]).astype(BF16)
    qkv_s[...] = (_dot(h, wqkv_ref[...]) + bqkv_ref[...]).astype(BF16)

    row = lax.broadcasted_iota(jnp.int32, (L, L), 0)
    col = lax.broadcasted_iota(jnp.int32, (L, L), 1)
    upper = col > row
    lo_half = lax.broadcasted_iota(jnp.int32, (L, LANES), 1) < ATTN_HEAD_DIM
    no_prev = jnp.where(first_step, NEG_INF, 0.0)

    for i in range(ts // L):
        r0 = i * L
        for kv in range(ATTN_KV_HEADS):
            k_cur = qkv_s[r0:r0 + L, nq + kv * LANES:nq + (kv + 1) * LANES]
            v_cur = qkv_s[r0:r0 + L, nq + kw + kv * LANES:nq + kw + (kv + 1) * LANES]
            if i == 0:
                k_prev = kprev_s[kv]
                v_prev = vprev_s[kv]
            else:
                k_prev = qkv_s[r0 - L:r0, nq + kv * LANES:nq + (kv + 1) * LANES]
                v_prev = qkv_s[r0 - L:r0, nq + kw + kv * LANES:nq + kw + (kv + 1) * LANES]
            kb = jnp.concatenate([k_prev, k_cur], axis=0)
            vb = jnp.concatenate([v_prev, v_cur], axis=0)
            for p in range(ATTN_HEADS // ATTN_KV_HEADS // 2):
                c0 = (kv * (ATTN_HEADS // ATTN_KV_HEADS // 2) + p) * LANES
                qp = qkv_s[r0:r0 + L, c0:c0 + LANES]
                zero = jnp.zeros_like(qp)
                outs = []
                for half in range(2):
                    hh = c0 // ATTN_HEAD_DIM + half
                    qm = jnp.where(lo_half, qp, zero) if half == 0 else jnp.where(lo_half, zero, qp)
                    l2 = _dot_nt(qm, kb)
                    lg = jnp.where(upper, l2[:, :L], l2[:, L:]) + bias_s[hh]
                    if i == 0:
                        lg = lg + jnp.where(upper, no_prev, 0.0)
                    sink = sinks_ref[hh]
                    m = jnp.maximum(jnp.max(lg, axis=-1, keepdims=True), sink)
                    e = jnp.exp(lg - m)
                    den = jnp.sum(e, axis=-1, keepdims=True) + jnp.exp(sink - m)
                    ez = jnp.zeros_like(e)
                    p2 = jnp.concatenate(
                        [jnp.where(upper, e, ez), jnp.where(upper, ez, e)], axis=1).astype(BF16)
                    outs.append(_dot(p2, vb) / den)
                att_s[r0:r0 + L, c0:c0 + LANES] = jnp.where(lo_half, outs[0], outs[1]).astype(BF16)
    for kv in range(ATTN_KV_HEADS):
        kprev_s[kv] = qkv_s[ts - L:ts, nq + kv * LANES:nq + (kv + 1) * LANES]
        vprev_s[kv] = qkv_s[ts - L:ts, nq + kw + kv * LANES:nq + kw + (kv + 1) * LANES]

    mix = _dot(att_s[...], wo_ref[...]) + bo_ref[...]
    o_ref[0] = x + g_ref[0] * mix


def _attn(x, sh, sc, g, nw, w_qkv, b_qkv, w_o, b_o, sinks, rel_table):
    bsz, seq, d = x.shape
    ts = TOKENS_PER_STEP
    nq = ATTN_HEADS * ATTN_HEAD_DIM
    nkv = ATTN_KV_HEADS * ATTN_HEAD_DIM
    scale = ATTN_HEAD_DIM ** -0.5

    def dup(a):
        lead = a.shape[:-1]
        a = a.reshape(lead + (ATTN_KV_HEADS, 1, ATTN_HEAD_DIM))
        return jnp.broadcast_to(a, lead + (ATTN_KV_HEADS, 2, ATTN_HEAD_DIM)).reshape(lead + (2 * nkv,))

    wq, wk, wv = w_qkv[:, :nq] * scale, w_qkv[:, nq:nq + nkv], w_qkv[:, nq + nkv:]
    bq, bk, bv = b_qkv[:nq] * scale, b_qkv[nq:nq + nkv], b_qkv[nq + nkv:]
    wqkv = jnp.concatenate([wq, dup(wk), dup(wv)], axis=1).astype(BF16)
    bqkv = jnp.concatenate([bq, dup(bk), dup(bv)]).reshape(1, -1)
    width = wqkv.shape[1]
    consts = [nw.reshape(1, d), wqkv, bqkv, w_o.astype(BF16), b_o.reshape(1, d),
              jnp.asarray(_folded_buckets())]
    row_spec = pl.BlockSpec((1, 1, d), lambda b, s: (b, 0, 0))
    x_spec = pl.BlockSpec((1, ts, d), lambda b, s: (b, s, 0))
    smem = pl.BlockSpec(memory_space=pltpu.SMEM)
    return pl.pallas_call(
        _attn_kernel,
        grid=(bsz, seq // ts),
        in_specs=[smem, smem, x_spec, row_spec, row_spec, row_spec]
        + [_const_spec(a.shape) for a in consts],
        out_specs=x_spec,
        out_shape=jax.ShapeDtypeStruct(x.shape, F32),
        scratch_shapes=[
            pltpu.VMEM((ATTN_HEADS, CHUNK, CHUNK), F32),
            pltpu.VMEM((ATTN_KV_HEADS, CHUNK, LANES), BF16),
            pltpu.VMEM((ATTN_KV_HEADS, CHUNK, LANES), BF16),
            pltpu.VMEM((ts, width), BF16),
            pltpu.VMEM((ts, nq), BF16),
        ],
        compiler_params=pltpu.CompilerParams(
            dimension_semantics=("arbitrary", "arbitrary"),
            vmem_limit_bytes=VMEM_LIMIT_BYTES),
        name="mixer_swa",
    )(sinks, rel_table.reshape(-1), x, sh, sc, g, *consts)


def _hidden_splits(hidden, max_cols):
    bounds, lo = [], 0
    while lo < hidden:
        hi = min(hidden, lo + max_cols)
        bounds.append((lo, hi))
        lo = hi
    return bounds


def _ffn_kernel(x_ref, sh_ref, sc_ref, g_ref, nw_ref, wg_ref, wu_ref, wd_ref, fnw_ref, o_ref,
                *, final_norm, splits):
    rows = x_ref.shape[1] // FFN_ROW_SUBBLOCKS
    for r in range(FFN_ROW_SUBBLOCKS):
        x = x_ref[0, r * rows:(r + 1) * rows, :]
        h = _norm_mod(x, nw_ref[...], sc_ref[0], sh_ref[0]).astype(BF16)
        acc = None
        for lo, hi in splits:
            a = _dot(h, wg_ref[:, lo:hi])
            b = _dot(h, wu_ref[:, lo:hi])
            part = _dot((_silu(a) * b).astype(BF16), wd_ref[lo:hi, :])
            acc = part if acc is None else acc + part
        out = x + g_ref[0] * acc
        if final_norm:
            out = _rms(out) * fnw_ref[...]
        o_ref[0, r * rows:(r + 1) * rows, :] = out


def _ffn(x, sh, sc, g, nw, w_gate, w_up, w_down, final_w, final_norm):
    bsz, seq, d = x.shape
    ts = FFN_TOKENS_PER_STEP
    splits = _hidden_splits(w_gate.shape[1], FFN_HIDDEN_COLS_PER_PASS)
    consts = [nw.reshape(1, d), w_gate.astype(BF16), w_up.astype(BF16), w_down.astype(BF16),
              final_w.reshape(1, d)]
    row_spec = pl.BlockSpec((1, 1, d), lambda b, s: (b, 0, 0))
    x_spec = pl.BlockSpec((1, ts, d), lambda b, s: (b, s, 0))
    return pl.pallas_call(
        functools.partial(_ffn_kernel, final_norm=final_norm, splits=splits),
        grid=(bsz, seq // ts),
        in_specs=[x_spec, row_spec, row_spec, row_spec] + [_const_spec(a.shape) for a in consts],
        out_specs=x_spec,
        out_shape=jax.ShapeDtypeStruct(x.shape, F32),
        compiler_params=pltpu.CompilerParams(
            dimension_semantics=("parallel", "parallel"),
            vmem_limit_bytes=VMEM_LIMIT_BYTES),
        name="ffn_swiglu",
    )(x, sh, sc, g, *consts)


def kernel(x, c, ada_w, ada_b, norm_mix_w, norm_ffn_w, in_w_even, conv_w, conv_b, dt_bias, a_log, d_skip, ssm_norm_w, gmlp_ln_w, gmlp_ln_b, gmlp_ws, gmlp_bs, out_w_even, qkv_w, qkv_b, o_w, o_b, sinks, rel_table, ffn_gate_w, ffn_up_w, ffn_down_w, final_norm_w):
    bsz, seq, d = x.shape
    depth = ada_w.shape[0]
    mods = _mods(c, ada_w, ada_b)
    for layer in range(depth):
        m = mods[layer].reshape(bsz, 6, 1, d)
        sh1, sc1, g1, sh2, sc2, g2 = (m[:, j] for j in range(6))
        i = layer // 2
        if layer % 2 == 0:
            x = _mixer0(x, sh1, sc1, g1, norm_mix_w[layer], in_w_even[i], conv_w[i], conv_b[i],
                        dt_bias[i], a_log[i], d_skip[i], ssm_norm_w[i], gmlp_ln_w[i], gmlp_ln_b[i],
                        gmlp_ws[i], gmlp_bs[i], out_w_even[i])
        else:
            x = _attn(x, sh1, sc1, g1, norm_mix_w[layer], qkv_w[i], qkv_b[i], o_w[i], o_b[i],
                      sinks[i], rel_table)
        x = _ffn(x, sh2, sc2, g2, norm_ffn_w[layer], ffn_gate_w[layer], ffn_up_w[layer],
                 ffn_down_w[layer], final_norm_w, final_norm=(layer == depth - 1))
    return x
```

```python
import functools
import math

import numpy as np
import jax
import jax.numpy as jnp
from jax import lax
from jax.experimental import pallas as pl
from jax.experimental.pallas import tpu as pltpu

F32 = jnp.float32
BF16 = jnp.bfloat16

EPS = 1e-6
NEG_INF = -1e30

SSM_HEADS = 16
SSM_HEAD_DIM = 64
SSM_GROUPS = 2
SSM_STATE = 128
SSM_CONV = 4
GMLP_GROUPS = 8
ATTN_HEADS = 16
ATTN_KV_HEADS = 2
ATTN_HEAD_DIM = 64
REL_BUCKETS = 32
REL_MAX_DIST = 128
CHUNK = 128

LANES = 128
SUBLANES = 8
VMEM_LIMIT_BYTES = 56 * 1024 * 1024

MXU_TILE = 256

TOKENS_PER_STEP = 512
SSD_TOKENS_PER_STEP = 1024
MIXER_ROWS_PER_SUBBLOCK = 256
MIXER_SLOTS = 2
MIXER_COLS_PER_TASK = MXU_TILE
FFN_TOKENS_PER_STEP = 512
FFN_HIDDEN_COLS_PER_PASS = 6 * MXU_TILE
FFN_ROW_SUBBLOCKS = 2


def _sigmoid(x):
    return jax.nn.sigmoid(x)


def _silu(x):
    return x * _sigmoid(x)


def _gelu(x):
    return 0.5 * x * (1.0 + lax.erf(x * (1.0 / math.sqrt(2.0))))


def _softplus(x):
    return jnp.maximum(x, 0.0) + jnp.log(1.0 + jnp.exp(-jnp.abs(x)))


def _rms(x):
    return x * lax.rsqrt(jnp.mean(x * x, axis=-1, keepdims=True) + EPS)


def _norm_mod(x, nw, sc, sh):
    return (_rms(x) * nw) * (1.0 + sc) + sh


_dot = functools.partial(jnp.dot, preferred_element_type=F32)


def _dot_nt(a, b):
    return lax.dot_general(a, b, (((1,), (1,)), ((), ())), preferred_element_type=F32)


def _mods_kernel(ct_ref, w_ref, b_ref, o_ref):
    ct = ct_ref[...]
    cond = _silu(ct)
    w = w_ref[0]
    for b in range(ct.shape[1]):
        o_ref[0, b:b + 1, :] = jnp.sum(w * cond[:, b:b + 1], axis=0, keepdims=True) + b_ref[0]


def _mods(c, ada_w, ada_b):
    depth, d, n = ada_w.shape
    bsz = c.shape[0]
    tn = 1536
    return pl.pallas_call(
        _mods_kernel,
        grid=(depth, n // tn),
        in_specs=[
            pl.BlockSpec((d, bsz), lambda l, j: (0, 0)),
            pl.BlockSpec((1, d, tn), lambda l, j: (l, 0, j)),
            pl.BlockSpec((1, 1, tn), lambda l, j: (l, 0, j)),
        ],
        out_specs=pl.BlockSpec((1, bsz, tn), lambda l, j: (l, 0, j)),
        out_shape=jax.ShapeDtypeStruct((depth, bsz, n), F32),
        compiler_params=pltpu.CompilerParams(
            dimension_semantics=("arbitrary", "arbitrary"),
            vmem_limit_bytes=VMEM_LIMIT_BYTES),
        name="adaln_mods",
    )(c.T, ada_w, ada_b.reshape(depth, 1, n))


def _mixer0_kernel(x_ref, xnext_ref, sh_ref, sc_ref, g_ref, nw_ref,
                   win_ref,
                   cw_ref, cb_ref, dtb_ref, alog_ref, dsk_ref, snw_ref,
                   lnw_ref, lnb_ref, ws_ref, bst_ref, wout_ref, e_ref,
                   o_ref,
                   state_s, *slot_scratch):
    ts = x_ref.shape[1]
    L = CHUNK
    per_slot = len(slot_scratch) // MIXER_SLOTS
    slots = [slot_scratch[i * per_slot:(i + 1) * per_slot] for i in range(MIXER_SLOTS)]
    rb, d = slots[0][0].shape
    inner = slots[0][3].shape[1]
    ginner = slots[0][4].shape[1]
    gw = inner // SSM_GROUPS
    n_slab = slots[0][1].shape[0]
    n_xs = inner // LANES
    c_u = inner + n_slab * LANES
    c_v = c_u + ginner
    c_dt = c_v + ginner
    n_sub = ts // rb
    gd = ginner // GMLP_GROUPS

    row = lax.broadcasted_iota(jnp.int32, (L, L), 0)
    col = lax.broadcasted_iota(jnp.int32, (L, L), 1)
    causal = col <= row
    tril_b = jnp.where(causal, 1.0, 0.0).astype(BF16)
    lo_half = lax.broadcasted_iota(jnp.int32, (L, LANES), 1) < SSM_HEAD_DIM
    head_lane = lax.broadcasted_iota(jnp.int32, (1, LANES), 1) < SSM_HEADS
    a_neg = jnp.where(head_lane, -jnp.exp(alog_ref[...]), 0.0)
    ws_pairs = [
        jnp.concatenate([jnp.where(causal, ws_ref[2 * gp], 0.0),
                         jnp.where(causal, ws_ref[2 * gp + 1], 0.0)], axis=1).astype(BF16)
        for gp in range(GMLP_GROUPS // 2)]

    pending = []

    def fill():
        if pending:
            pending.pop(0)()

    def project_tasks(k):
        hb_s, xpad_s, _, z_s, u_s, v_s, dt_s, _ = slots[k % MIXER_SLOTS]

        def head():
            xk = xnext_ref[0] if k == n_sub else x_ref[0, k * rb:(k + 1) * rb, :]
            hb_s[...] = _norm_mod(xk, nw_ref[...], sc_ref[0], sh_ref[0]).astype(BF16)
            for j in range(n_slab):
                if k == 0:
                    xpad_s[j, 0:SUBLANES, :] = jnp.zeros((SUBLANES, LANES), F32)
                else:
                    prev_xpad = slots[(k - 1) % MIXER_SLOTS][1]
                    xpad_s[j, 0:SUBLANES, :] = prev_xpad[j, rb:rb + SUBLANES, :]

        def to_cols(dst, c_src, c0, c1):
            def run():
                dst[:, c0:c1] = _dot(hb_s[...], win_ref[:, c_src + c0:c_src + c1])
            return run

        def to_slabs(j0, j1):
            def run():
                r = _dot(hb_s[...], win_ref[:, inner + j0 * LANES:inner + j1 * LANES])
                for j in range(j0, j1):
                    xpad_s[j, SUBLANES:SUBLANES + rb, :] = r[:, (j - j0) * LANES:(j - j0 + 1) * LANES]
            return run

        step = MIXER_COLS_PER_TASK
        tasks = [head]
        tasks += [to_cols(z_s, 0, c0, c0 + step) for c0 in range(0, inner, step)]
        tasks += [to_slabs(j0, j0 + step // LANES) for j0 in range(0, n_slab, step // LANES)]
        tasks += [to_cols(u_s, c_u, c0, c0 + step) for c0 in range(0, ginner, step)]
        tasks += [to_cols(v_s, c_v, c0, c0 + step) for c0 in range(0, ginner, step)]
        tasks.append(to_cols(dt_s, c_dt, 0, LANES))
        return tasks

    def out_tasks(k):
        ycat_s = slots[k % MIXER_SLOTS][7]
        rs = slice(k * rb, (k + 1) * rb)

        def cols(c0, c1):
            def run():
                mix = _dot(ycat_s[...], wout_ref[:, c0:c1])
                o_ref[0, rs, c0:c1] = x_ref[0, rs, c0:c1] + g_ref[0][:, c0:c1] * mix
            return run

        return [cols(c0, c0 + MXU_TILE) for c0 in range(0, d, MXU_TILE)]

    def chunk_work(k, ci):
        _, xpad_s, xact_s, z_s, u_s, v_s, dt_s, ycat_s = slots[k % MIXER_SLOTS]
        r0 = ci * L
        for j in range(n_slab):
            cw = cw_ref[:, j * LANES:(j + 1) * LANES]
            base = r0 + SUBLANES - (SSM_CONV - 1)
            acc = cb_ref[:, j * LANES:(j + 1) * LANES] + cw[0:1] * xpad_s[j, base:base + L, :]
            for t in range(1, SSM_CONV):
                acc = acc + cw[t:t + 1] * xpad_s[j, base + t:base + t + L, :]
            xact_s[j, r0:r0 + L, :] = _silu(acc)
        fill()
        xs = jnp.concatenate([xact_s[j, r0:r0 + L, :] for j in range(n_xs)], axis=1)

        dt = _softplus(dt_s[r0:r0 + L, :] + dtb_ref[...])
        la = dt * a_neg
        la_hi = la.astype(BF16)
        la_lo = (la - la_hi.astype(F32)).astype(BF16)
        cum2 = _dot(tril_b, jnp.concatenate([la_hi, la_lo], axis=1))
        cum = cum2[:, :LANES] + cum2[:, LANES:]
        fill()
        cum_t = cum.T
        dt_t = dt.T
        cum_last = cum[L - 1:L, :]
        cdec = jnp.broadcast_to(jnp.exp(cum_last), (16, LANES))
        cdec_hi = cdec.astype(BF16)
        cdec_lo = (cdec - cdec_hi.astype(F32)).astype(BF16)
        small = jnp.concatenate(
            [jnp.exp(cum).astype(BF16),
             (dt * jnp.exp(cum_last - cum)).astype(BF16),
             cdec_hi, cdec_lo], axis=0)
        spread = _dot(small, e_ref[...])
        ecum_x = spread[0:L]
        w_x = spread[L:2 * L]
        cdec_x = spread[2 * L:2 * L + 1] + spread[2 * L + 16:2 * L + 17]
        fill()
        xsb = xs.astype(BF16)
        xdw = (xs * w_x).astype(BF16)

        y_parts = []
        for g in range(SSM_GROUPS):
            bg_t = xact_s[n_xs + g, r0:r0 + L, :].T.astype(BF16)
            cg = xact_s[n_xs + SSM_GROUPS + g, r0:r0 + L, :].astype(BF16)
            cb = _dot(cg, bg_t)
            prev = state_s[g]
            y_off = _dot(cg, prev.astype(BF16)) * ecum_x[:, g * gw:(g + 1) * gw]
            new = _dot(bg_t, xdw[:, g * gw:(g + 1) * gw])
            state_s[g] = prev * cdec_x[:, g * gw:(g + 1) * gw] + new
            fill()
            pairs = []
            for p in range(gw // LANES):
                h0 = g * (SSM_HEADS // SSM_GROUPS) + 2 * p
                ms = []
                for hh in (h0, h0 + 1):
                    seg = cum[:, hh:hh + 1] - cum_t[hh:hh + 1, :]
                    dec = jnp.exp(jnp.where(causal, seg, NEG_INF))
                    ms.append((cb * dec * dt_t[hh:hh + 1, :]).astype(BF16))
                lhs = jnp.concatenate(ms, axis=1)
                xp = xsb[:, h0 * SSM_HEAD_DIM:h0 * SSM_HEAD_DIM + LANES]
                zero = jnp.zeros_like(xp)
                rhs = jnp.concatenate(
                    [jnp.where(lo_half, xp, zero), jnp.where(lo_half, zero, xp)], axis=0)
                pairs.append(_dot(lhs, rhs))
                if p % 2 == 1:
                    fill()
            y_parts.append(jnp.concatenate(pairs, axis=1) + y_off)
        y = jnp.concatenate(y_parts, axis=1) + dsk_ref[...] * xs

        y = y * _silu(z_s[r0:r0 + L, :])
        y = jnp.concatenate([_rms(y[:, g * gw:(g + 1) * gw]) for g in range(SSM_GROUPS)], axis=1)
        ycat_s[r0:r0 + L, 0:inner] = (y * snw_ref[...]).astype(BF16)
        fill()

        u = _gelu(u_s[r0:r0 + L, :])
        v = _gelu(v_s[r0:r0 + L, :])
        mu = jnp.mean(v, axis=-1, keepdims=True)
        vc = v - mu
        var = jnp.mean(vc * vc, axis=-1, keepdims=True)
        vn = ((vc * lax.rsqrt(var + EPS)) * lnw_ref[...] + lnb_ref[...]).astype(BF16)
        for gp in range(GMLP_GROUPS // 2):
            va = vn[:, 2 * gp * gd:(2 * gp + 1) * gd]
            vb = vn[:, (2 * gp + 1) * gd:(2 * gp + 2) * gd]
            vz = jnp.zeros_like(va)
            rhs = jnp.concatenate([jnp.concatenate([va, vz], axis=1),
                                   jnp.concatenate([vz, vb], axis=1)], axis=0)
            sv2 = _dot(ws_pairs[gp], rhs)
            for half in range(2):
                g = 2 * gp + half
                sv = sv2[:, half * gd:(half + 1) * gd] + bst_ref[:, g:g + 1]
                ycat_s[r0:r0 + L, inner + g * gd:inner + (g + 1) * gd] = (
                    u[:, g * gd:(g + 1) * gd] * sv).astype(BF16)
            if gp % 2 == 1:
                fill()

    assert n_sub % MIXER_SLOTS == 0

    @pl.when(pl.program_id(1) == 0)
    def _sequence_start():
        state_s[...] = jnp.zeros_like(state_s)
        for task in project_tasks(0):
            task()

    for k in range(n_sub):
        if k >= 1:
            pending.extend(out_tasks(k - 1))
        pending.extend(project_tasks(k + 1))
        for ci in range(rb // L):
            chunk_work(k, ci)
        while pending:
            fill()
    for task in out_tasks(n_sub - 1):
        task()


def _const_spec(shape):
    nd = len(shape)
    return pl.BlockSpec(shape, lambda b, s: (0,) * nd, pipeline_mode=pl.Buffered(1))


def _mixer0(x, sh, sc, g, nw, in_w, conv_w, conv_b, dt_bias, a_log, d_skip, ssm_norm_w,
            ln_w, ln_b, w_s, b_s, out_w):
    bsz, seq, d = x.shape
    ts = SSD_TOKENS_PER_STEP
    rb = MIXER_ROWS_PER_SUBBLOCK
    inner = SSM_HEADS * SSM_HEAD_DIM
    conv_dim = inner + 2 * SSM_GROUPS * SSM_STATE
    ginner = w_s.shape[0] * w_s.shape[1]
    o1 = inner
    o2 = o1 + conv_dim
    o3 = o2 + SSM_HEADS
    w_in = jnp.concatenate(
        [in_w[:, :o2], in_w[:, o3:],
         jnp.pad(in_w[:, o2:o3], ((0, 0), (0, LANES - SSM_HEADS)))], axis=1).astype(BF16)
    pad_h = lambda a: jnp.pad(a.reshape(1, SSM_HEADS), ((0, 0), (0, LANES - SSM_HEADS)))
    e_np = np.zeros((LANES, inner), np.float32)
    for hh in range(SSM_HEADS):
        e_np[hh, hh * SSM_HEAD_DIM:(hh + 1) * SSM_HEAD_DIM] = 1.0
    consts = [
        nw.reshape(1, d), w_in,
        conv_w, conv_b.reshape(1, conv_dim), pad_h(dt_bias), pad_h(a_log),
        jnp.repeat(d_skip, SSM_HEAD_DIM).reshape(1, inner), ssm_norm_w.reshape(1, inner),
        ln_w.reshape(1, ginner), ln_b.reshape(1, ginner), w_s, b_s.T,
        out_w.astype(BF16), jnp.asarray(e_np, BF16),
    ]
    slot_scratch = [
        pltpu.VMEM((rb, d), BF16),
        pltpu.VMEM((conv_dim // LANES, rb + SUBLANES, LANES), F32),
        pltpu.VMEM((conv_dim // LANES, rb, LANES), F32),
        pltpu.VMEM((rb, inner), F32),
        pltpu.VMEM((rb, ginner), F32),
        pltpu.VMEM((rb, ginner), F32),
        pltpu.VMEM((rb, LANES), F32),
        pltpu.VMEM((rb, inner + ginner), BF16),
    ]
    row_spec = pl.BlockSpec((1, 1, d), lambda b, s: (b, 0, 0))
    x_spec = pl.BlockSpec((1, ts, d), lambda b, s: (b, s, 0))
    last_sub = seq // rb - 1
    xnext_spec = pl.BlockSpec(
        (1, rb, d), lambda b, s: (b, jnp.minimum((s + 1) * (ts // rb), last_sub), 0))
    return pl.pallas_call(
        _mixer0_kernel,
        grid=(bsz, seq // ts),
        in_specs=[x_spec, xnext_spec, row_spec, row_spec, row_spec]
        + [_const_spec(a.shape) for a in consts],
        out_specs=x_spec,
        out_shape=jax.ShapeDtypeStruct(x.shape, F32),
        scratch_shapes=[
            pltpu.VMEM((SSM_GROUPS, SSM_STATE, inner // SSM_GROUPS), F32),
        ] + slot_scratch * MIXER_SLOTS,
        compiler_params=pltpu.CompilerParams(
            dimension_semantics=("arbitrary", "arbitrary"),
            vmem_limit_bytes=VMEM_LIMIT_BYTES),
        name="mixer_ssd_gmlp",
    )(x, x, sh, sc, g, *consts)


def _folded_buckets():
    dist = (np.arange(CHUNK)[:, None] - np.arange(CHUNK)[None, :]) % CHUNK
    max_exact = REL_BUCKETS // 2
    log_ratio = (np.log(np.maximum(dist, 1).astype(np.float32) / max_exact)
                 / math.log(REL_MAX_DIST / max_exact))
    large = max_exact + (log_ratio * (REL_BUCKETS - max_exact)).astype(np.int32)
    return np.where(dist < max_exact, dist, np.minimum(large, REL_BUCKETS - 1)).astype(np.int32)


def _attn_kernel(sinks_ref, tab_ref, x_ref, sh_ref, sc_ref, g_ref, nw_ref,
                 wqkv_ref, bqkv_ref, wo_ref, bo_ref, bucket_ref,
                 o_ref,
                 bias_s, kprev_s, vprev_s, qkv_s, att_s):
    ts = x_ref.shape[1]
    L = CHUNK
    nq = ATTN_HEADS * ATTN_HEAD_DIM
    kw = ATTN_KV_HEADS * LANES
    first_step = pl.program_id(1) == 0

    @pl.when(jnp.logical_and(pl.program_id(0) == 0, first_step))
    def _build_bias():
        bucket = bucket_ref[...]
        for hh in range(ATTN_HEADS):
            acc = jnp.zeros((L, L), F32)
            for b in range(REL_BUCKETS):
                acc = jnp.where(bucket == b, tab_ref[b * ATTN_HEADS + hh], acc)
            bias_s[hh] = acc

    @pl.when(first_step)
    def _init():
        kprev_s[...] = jnp.zeros_like(kprev_s)
        vprev_s[...] = jnp.zeros_like(vprev_s)

    x = x_ref[0]
    h = _norm_mod(x, nw_ref[...], sc_ref[0], sh_ref[0]).astype(BF16)
    qkv_s[...] = (_dot(h, wqkv_ref[...]) + bqkv_ref[...]).astype(BF16)

    row = lax.broadcasted_iota(jnp.int32, (L, L), 0)
    col = lax.broadcasted_iota(jnp.int32, (L, L), 1)
    upper = col > row
    lo_half = lax.broadcasted_iota(jnp.int32, (L, LANES), 1) < ATTN_HEAD_DIM
    no_prev = jnp.where(first_step, NEG_INF, 0.0)

    for i in range(ts // L):
        r0 = i * L
        for kv in range(ATTN_KV_HEADS):
            k_cur = qkv_s[r0:r0 + L, nq + kv * LANES:nq + (kv + 1) * LANES]
            v_cur = qkv_s[r0:r0 + L, nq + kw + kv * LANES:nq + kw + (kv + 1) * LANES]
            if i == 0:
                k_prev = kprev_s[kv]
                v_prev = vprev_s[kv]
            else:
                k_prev = qkv_s[r0 - L:r0, nq + kv * LANES:nq + (kv + 1) * LANES]
                v_prev = qkv_s[r0 - L:r0, nq + kw + kv * LANES:nq + kw + (kv + 1) * LANES]
            kb = jnp.concatenate([k_prev, k_cur], axis=0)
            vb = jnp.concatenate([v_prev, v_cur], axis=0)
            for p in range(ATTN_HEADS // ATTN_KV_HEADS // 2):
                c0 = (kv * (ATTN_HEADS // ATTN_KV_HEADS // 2) + p) * LANES
                qp = qkv_s[r0:r0 + L, c0:c0 + LANES]
                zero = jnp.zeros_like(qp)
                outs = []
                for half in range(2):
                    hh = c0 // ATTN_HEAD_DIM + half
                    qm = jnp.where(lo_half, qp, zero) if half == 0 else jnp.where(lo_half, zero, qp)
                    l2 = _dot_nt(qm, kb)
                    lg = jnp.where(upper, l2[:, :L], l2[:, L:]) + bias_s[hh]
                    if i == 0:
                        lg = lg + jnp.where(upper, no_prev, 0.0)
                    sink = sinks_ref[hh]
                    m = jnp.maximum(jnp.max(lg, axis=-1, keepdims=True), sink)
                    e = jnp.exp(lg - m)
                    den = jnp.sum(e, axis=-1, keepdims=True) + jnp.exp(sink - m)
                    ez = jnp.zeros_like(e)
                    p2 = jnp.concatenate(
                        [jnp.where(upper, e, ez), jnp.where(upper, ez, e)], axis=1).astype(BF16)
                    outs.append(_dot(p2, vb) / den)
                att_s[r0:r0 + L, c0:c0 + LANES] = jnp.where(lo_half, outs[0], outs[1]).astype(BF16)
    for kv in range(ATTN_KV_HEADS):
        kprev_s[kv] = qkv_s[ts - L:ts, nq + kv * LANES:nq + (kv + 1) * LANES]
        vprev_s[kv] = qkv_s[ts - L:ts, nq + kw + kv * LANES:nq + kw + (kv + 1) * LANES]

    mix = _dot(att_s[...], wo_ref[...]) + bo_ref[...]
    o_ref[0] = x + g_ref[0] * mix


def _attn(x, sh, sc, g, nw, w_qkv, b_qkv, w_o, b_o, sinks, rel_table):
    bsz, seq, d = x.shape
    ts = TOKENS_PER_STEP
    nq = ATTN_HEADS * ATTN_HEAD_DIM
    nkv = ATTN_KV_HEADS * ATTN_HEAD_DIM
    scale = ATTN_HEAD_DIM ** -0.5

    def dup(a):
        lead = a.shape[:-1]
        a = a.reshape(lead + (ATTN_KV_HEADS, 1, ATTN_HEAD_DIM))
        return jnp.broadcast_to(a, lead + (ATTN_KV_HEADS, 2, ATTN_HEAD_DIM)).reshape(lead + (2 * nkv,))

    wq, wk, wv = w_qkv[:, :nq] * scale, w_qkv[:, nq:nq + nkv], w_qkv[:, nq + nkv:]
    bq, bk, bv = b_qkv[:nq] * scale, b_qkv[nq:nq + nkv], b_qkv[nq + nkv:]
    wqkv = jnp.concatenate([wq, dup(wk), dup(wv)], axis=1).astype(BF16)
    bqkv = jnp.concatenate([bq, dup(bk), dup(bv)]).reshape(1, -1)
    width = wqkv.shape[1]
    consts = [nw.reshape(1, d), wqkv, bqkv, w_o.astype(BF16), b_o.reshape(1, d),
              jnp.asarray(_folded_buckets())]
    row_spec = pl.BlockSpec((1, 1, d), lambda b, s: (b, 0, 0))
    x_spec = pl.BlockSpec((1, ts, d), lambda b, s: (b, s, 0))
    smem = pl.BlockSpec(memory_space=pltpu.SMEM)
    return pl.pallas_call(
        _attn_kernel,
        grid=(bsz, seq // ts),
        in_specs=[smem, smem, x_spec, row_spec, row_spec, row_spec]
        + [_const_spec(a.shape) for a in consts],
        out_specs=x_spec,
        out_shape=jax.ShapeDtypeStruct(x.shape, F32),
        scratch_shapes=[
            pltpu.VMEM((ATTN_HEADS, CHUNK, CHUNK), F32),
            pltpu.VMEM((ATTN_KV_HEADS, CHUNK, LANES), BF16),
            pltpu.VMEM((ATTN_KV_HEADS, CHUNK, LANES), BF16),
            pltpu.VMEM((ts, width), BF16),
            pltpu.VMEM((ts, nq), BF16),
        ],
        compiler_params=pltpu.CompilerParams(
            dimension_semantics=("arbitrary", "arbitrary"),
            vmem_limit_bytes=VMEM_LIMIT_BYTES),
        name="mixer_swa",
    )(sinks, rel_table.reshape(-1), x, sh, sc, g, *consts)


def _hidden_splits(hidden, max_cols):
    bounds, lo = [], 0
    while lo < hidden:
        hi = min(hidden, lo + max_cols)
        bounds.append((lo, hi))
        lo = hi
    return bounds


def _ffn_kernel(x_ref, sh_ref, sc_ref, g_ref, nw_ref, wg_ref, wu_ref, wd_ref, fnw_ref, o_ref,
                *, final_norm, splits):
    rows = x_ref.shape[1] // FFN_ROW_SUBBLOCKS
    for r in range(FFN_ROW_SUBBLOCKS):
        x = x_ref[0, r * rows:(r + 1) * rows, :]
        h = _norm_mod(x, nw_ref[...], sc_ref[0], sh_ref[0]).astype(BF16)
        acc = None
        for lo, hi in splits:
            a = _dot(h, wg_ref[0, :, lo:hi])
            b = _dot(h, wu_ref[0, :, lo:hi])
            part = _dot((_silu(a) * b).astype(BF16), wd_ref[0, lo:hi, :])
            acc = part if acc is None else acc + part
        out = x + g_ref[0] * acc
        if final_norm:
            out = _rms(out) * fnw_ref[...]
        o_ref[0, r * rows:(r + 1) * rows, :] = out


def _ffn(x, sh, sc, g, nw, layer, w_gate, w_up, w_down, final_w, final_norm):
    bsz, seq, d = x.shape
    ts = FFN_TOKENS_PER_STEP
    splits = _hidden_splits(w_gate.shape[2], FFN_HIDDEN_COLS_PER_PASS)
    row_spec = pl.BlockSpec((1, 1, d), lambda b, s: (b, 0, 0))
    x_spec = pl.BlockSpec((1, ts, d), lambda b, s: (b, s, 0))

    def layer_spec(a):
        return pl.BlockSpec((1,) + a.shape[1:], lambda b, s: (layer, 0, 0),
                            pipeline_mode=pl.Buffered(1))

    nw2, fw2 = nw.reshape(1, d), final_w.reshape(1, d)
    return pl.pallas_call(
        functools.partial(_ffn_kernel, final_norm=final_norm, splits=splits),
        grid=(bsz, seq // ts),
        in_specs=[x_spec, row_spec, row_spec, row_spec, _const_spec(nw2.shape),
                  layer_spec(w_gate), layer_spec(w_up), layer_spec(w_down), _const_spec(fw2.shape)],
        out_specs=x_spec,
        out_shape=jax.ShapeDtypeStruct(x.shape, F32),
        compiler_params=pltpu.CompilerParams(
            dimension_semantics=("parallel", "parallel"),
            vmem_limit_bytes=VMEM_LIMIT_BYTES),
        name="ffn_swiglu",
    )(x, sh, sc, g, nw2, w_gate, w_up, w_down, fw2)


def kernel(x, c, ada_w, ada_b, norm_mix_w, norm_ffn_w, in_w_even, conv_w, conv_b, dt_bias, a_log, d_skip, ssm_norm_w, gmlp_ln_w, gmlp_ln_b, gmlp_ws, gmlp_bs, out_w_even, qkv_w, qkv_b, o_w, o_b, sinks, rel_table, ffn_gate_w, ffn_up_w, ffn_down_w, final_norm_w):
    bsz, seq, d = x.shape
    depth = ada_w.shape[0]
    mods = _mods(c, ada_w, ada_b)
    gate_b, up_b, down_b = (w.astype(BF16) for w in (ffn_gate_w, ffn_up_w, ffn_down_w))
    for layer in range(depth):
        m = mods[layer].reshape(bsz, 6, 1, d)
        sh1, sc1, g1, sh2, sc2, g2 = (m[:, j] for j in range(6))
        i = layer // 2
        if layer % 2 == 0:
            x = _mixer0(x, sh1, sc1, g1, norm_mix_w[layer], in_w_even[i], conv_w[i], conv_b[i],
                        dt_bias[i], a_log[i], d_skip[i], ssm_norm_w[i], gmlp_ln_w[i], gmlp_ln_b[i],
                        gmlp_ws[i], gmlp_bs[i], out_w_even[i])
        else:
            x = _attn(x, sh1, sc1, g1, norm_mix_w[layer], qkv_w[i], qkv_b[i], o_w[i], o_b[i],
                      sinks[i], rel_table)
        x = _ffn(x, sh2, sc2, g2, norm_ffn_w[layer], layer, gate_b, up_b, down_b,
                 final_norm_w, final_norm=(layer == depth - 1))
    return x
```

```python
import functools
import math

import numpy as np
import jax
import jax.numpy as jnp
from jax import lax
from jax.experimental import pallas as pl
from jax.experimental.pallas import tpu as pltpu

F32 = jnp.float32
BF16 = jnp.bfloat16

EPS = 1e-6
NEG_INF = -1e30

SSM_HEADS = 16
SSM_HEAD_DIM = 64
SSM_GROUPS = 2
SSM_STATE = 128
SSM_CONV = 4
GMLP_GROUPS = 8
ATTN_HEADS = 16
ATTN_KV_HEADS = 2
ATTN_HEAD_DIM = 64
REL_BUCKETS = 32
REL_MAX_DIST = 128
CHUNK = 128

LANES = 128
SUBLANES = 8
VMEM_LIMIT_BYTES = 56 * 1024 * 1024

MXU_TILE = 256

TOKENS_PER_STEP = 1024
SSD_TOKENS_PER_STEP = 1024
MIXER_ROWS_PER_SUBBLOCK = 256
MIXER_SLOTS = 2
MIXER_COLS_PER_TASK = MXU_TILE
FFN_TOKENS_PER_STEP = 1024
FFN_HIDDEN_COLS_PER_PASS = 6 * MXU_TILE
FFN_ROW_SUBBLOCKS = 4


def _sigmoid(x):
    return jax.nn.sigmoid(x)


def _silu(x):
    return x * _sigmoid(x)


def _gelu(x):
    return 0.5 * x * (1.0 + lax.erf(x * (1.0 / math.sqrt(2.0))))


def _softplus(x):
    return jnp.maximum(x, 0.0) + jnp.log(1.0 + jnp.exp(-jnp.abs(x)))


def _rms(x):
    return x * lax.rsqrt(jnp.mean(x * x, axis=-1, keepdims=True) + EPS)


def _norm_mod(x, nw, sc, sh):
    return (_rms(x) * nw) * (1.0 + sc) + sh


_dot = functools.partial(jnp.dot, preferred_element_type=F32)


def _dot_nt(a, b):
    return lax.dot_general(a, b, (((1,), (1,)), ((), ())), preferred_element_type=F32)


def _mods_kernel(ct_ref, w_ref, b_ref, o_ref):
    ct = ct_ref[...]
    cond = _silu(ct)
    w = w_ref[0]
    for b in range(ct.shape[1]):
        o_ref[0, b:b + 1, :] = jnp.sum(w * cond[:, b:b + 1], axis=0, keepdims=True) + b_ref[0]


def _mods(c, ada_w, ada_b):
    depth, d, n = ada_w.shape
    bsz = c.shape[0]
    tn = 1536
    return pl.pallas_call(
        _mods_kernel,
        grid=(depth, n // tn),
        in_specs=[
            pl.BlockSpec((d, bsz), lambda l, j: (0, 0)),
            pl.BlockSpec((1, d, tn), lambda l, j: (l, 0, j)),
            pl.BlockSpec((1, 1, tn), lambda l, j: (l, 0, j)),
        ],
        out_specs=pl.BlockSpec((1, bsz, tn), lambda l, j: (l, 0, j)),
        out_shape=jax.ShapeDtypeStruct((depth, bsz, n), F32),
        compiler_params=pltpu.CompilerParams(
            dimension_semantics=("arbitrary", "arbitrary"),
            vmem_limit_bytes=VMEM_LIMIT_BYTES),
        name="adaln_mods",
    )(c.T, ada_w, ada_b.reshape(depth, 1, n))


def _mixer0_kernel(x_ref, xnext_ref, sh_ref, sc_ref, g_ref, nw_ref,
                   win_ref,
                   cw_ref, cb_ref, dtb_ref, alog_ref, dsk_ref, snw_ref,
                   lnw_ref, lnb_ref, ws_ref, bst_ref, wout_ref, e_ref,
                   o_ref,
                   state_s, *slot_scratch):
    ts = x_ref.shape[1]
    L = CHUNK
    per_slot = len(slot_scratch) // MIXER_SLOTS
    slots = [slot_scratch[i * per_slot:(i + 1) * per_slot] for i in range(MIXER_SLOTS)]
    rb, d = slots[0][0].shape
    inner = slots[0][3].shape[1]
    ginner = slots[0][4].shape[1]
    gw = inner // SSM_GROUPS
    n_slab = slots[0][1].shape[0]
    n_xs = inner // LANES
    c_u = inner + n_slab * LANES
    c_v = c_u + ginner
    c_dt = c_v + ginner
    n_sub = ts // rb
    gd = ginner // GMLP_GROUPS

    row = lax.broadcasted_iota(jnp.int32, (L, L), 0)
    col = lax.broadcasted_iota(jnp.int32, (L, L), 1)
    causal = col <= row
    tril_b = jnp.where(causal, 1.0, 0.0).astype(BF16)
    lo_half = lax.broadcasted_iota(jnp.int32, (L, LANES), 1) < SSM_HEAD_DIM
    head_lane = lax.broadcasted_iota(jnp.int32, (1, LANES), 1) < SSM_HEADS
    a_neg = jnp.where(head_lane, -jnp.exp(alog_ref[...]), 0.0)
    ws_pairs = [
        jnp.concatenate([jnp.where(causal, ws_ref[2 * gp], 0.0),
                         jnp.where(causal, ws_ref[2 * gp + 1], 0.0)], axis=1).astype(BF16)
        for gp in range(GMLP_GROUPS // 2)]

    pending = []

    def fill():
        if pending:
            pending.pop(0)()

    def project_tasks(k):
        hb_s, xpad_s, _, z_s, u_s, v_s, dt_s, _ = slots[k % MIXER_SLOTS]

        def head():
            xk = xnext_ref[0] if k == n_sub else x_ref[0, k * rb:(k + 1) * rb, :]
            hb_s[...] = _norm_mod(xk, nw_ref[...], sc_ref[0], sh_ref[0]).astype(BF16)
            for j in range(n_slab):
                if k == 0:
                    xpad_s[j, 0:SUBLANES, :] = jnp.zeros((SUBLANES, LANES), F32)
                else:
                    prev_xpad = slots[(k - 1) % MIXER_SLOTS][1]
                    xpad_s[j, 0:SUBLANES, :] = prev_xpad[j, rb:rb + SUBLANES, :]

        def to_cols(dst, c_src, c0, c1):
            def run():
                dst[:, c0:c1] = _dot(hb_s[...], win_ref[:, c_src + c0:c_src + c1])
            return run

        def to_slabs(j0, j1):
            def run():
                r = _dot(hb_s[...], win_ref[:, inner + j0 * LANES:inner + j1 * LANES])
                for j in range(j0, j1):
                    xpad_s[j, SUBLANES:SUBLANES + rb, :] = r[:, (j - j0) * LANES:(j - j0 + 1) * LANES]
            return run

        step = MIXER_COLS_PER_TASK
        tasks = [head]
        tasks += [to_cols(z_s, 0, c0, c0 + step) for c0 in range(0, inner, step)]
        tasks += [to_slabs(j0, j0 + step // LANES) for j0 in range(0, n_slab, step // LANES)]
        tasks += [to_cols(u_s, c_u, c0, c0 + step) for c0 in range(0, ginner, step)]
        tasks += [to_cols(v_s, c_v, c0, c0 + step) for c0 in range(0, ginner, step)]
        tasks.append(to_cols(dt_s, c_dt, 0, LANES))
        return tasks

    def out_tasks(k):
        ycat_s = slots[k % MIXER_SLOTS][7]
        rs = slice(k * rb, (k + 1) * rb)

        def cols(c0, c1):
            def run():
                mix = _dot(ycat_s[...], wout_ref[:, c0:c1])
                o_ref[0, rs, c0:c1] = x_ref[0, rs, c0:c1] + g_ref[0][:, c0:c1] * mix
            return run

        return [cols(c0, c0 + MXU_TILE) for c0 in range(0, d, MXU_TILE)]

    def chunk_work(k, ci):
        _, xpad_s, xact_s, z_s, u_s, v_s, dt_s, ycat_s = slots[k % MIXER_SLOTS]
        r0 = ci * L
        for j in range(n_slab):
            cw = cw_ref[:, j * LANES:(j + 1) * LANES]
            base = r0 + SUBLANES - (SSM_CONV - 1)
            acc = cb_ref[:, j * LANES:(j + 1) * LANES] + cw[0:1] * xpad_s[j, base:base + L, :]
            for t in range(1, SSM_CONV):
                acc = acc + cw[t:t + 1] * xpad_s[j, base + t:base + t + L, :]
            xact_s[j, r0:r0 + L, :] = _silu(acc)
        fill()
        xs = jnp.concatenate([xact_s[j, r0:r0 + L, :] for j in range(n_xs)], axis=1)

        dt = _softplus(dt_s[r0:r0 + L, :] + dtb_ref[...])
        la = dt * a_neg
        la_hi = la.astype(BF16)
        la_lo = (la - la_hi.astype(F32)).astype(BF16)
        cum2 = _dot(tril_b, jnp.concatenate([la_hi, la_lo], axis=1))
        cum = cum2[:, :LANES] + cum2[:, LANES:]
        fill()
        cum_t = cum.T
        dt_t = dt.T
        cum_last = cum[L - 1:L, :]
        cdec = jnp.broadcast_to(jnp.exp(cum_last), (16, LANES))
        cdec_hi = cdec.astype(BF16)
        cdec_lo = (cdec - cdec_hi.astype(F32)).astype(BF16)
        small = jnp.concatenate(
            [jnp.exp(cum).astype(BF16),
             (dt * jnp.exp(cum_last - cum)).astype(BF16),
             cdec_hi, cdec_lo], axis=0)
        spread = _dot(small, e_ref[...])
        ecum_x = spread[0:L]
        w_x = spread[L:2 * L]
        cdec_x = spread[2 * L:2 * L + 1] + spread[2 * L + 16:2 * L + 17]
        fill()
        xsb = xs.astype(BF16)
        xdw = (xs * w_x).astype(BF16)

        y_parts = []
        for g in range(SSM_GROUPS):
            bg_t = xact_s[n_xs + g, r0:r0 + L, :].T.astype(BF16)
            cg = xact_s[n_xs + SSM_GROUPS + g, r0:r0 + L, :].astype(BF16)
            cb = _dot(cg, bg_t)
            prev = state_s[g]
            y_off = _dot(cg, prev.astype(BF16)) * ecum_x[:, g * gw:(g + 1) * gw]
            new = _dot(bg_t, xdw[:, g * gw:(g + 1) * gw])
            state_s[g] = prev * cdec_x[:, g * gw:(g + 1) * gw] + new
            fill()
            pairs = []
            for p in range(gw // LANES):
                h0 = g * (SSM_HEADS // SSM_GROUPS) + 2 * p
                ms = []
                for hh in (h0, h0 + 1):
                    seg = cum[:, hh:hh + 1] - cum_t[hh:hh + 1, :]
                    dec = jnp.exp(jnp.where(causal, seg, NEG_INF))
                    ms.append((cb * dec * dt_t[hh:hh + 1, :]).astype(BF16))
                lhs = jnp.concatenate(ms, axis=1)
                xp = xsb[:, h0 * SSM_HEAD_DIM:h0 * SSM_HEAD_DIM + LANES]
                zero = jnp.zeros_like(xp)
                rhs = jnp.concatenate(
                    [jnp.where(lo_half, xp, zero), jnp.where(lo_half, zero, xp)], axis=0)
                pairs.append(_dot(lhs, rhs))
                if p % 2 == 1:
                    fill()
            y_parts.append(jnp.concatenate(pairs, axis=1) + y_off)
        y = jnp.concatenate(y_parts, axis=1) + dsk_ref[...] * xs

        y = y * _silu(z_s[r0:r0 + L, :])
        y = jnp.concatenate([_rms(y[:, g * gw:(g + 1) * gw]) for g in range(SSM_GROUPS)], axis=1)
        ycat_s[r0:r0 + L, 0:inner] = (y * snw_ref[...]).astype(BF16)
        fill()

        u = _gelu(u_s[r0:r0 + L, :])
        v = _gelu(v_s[r0:r0 + L, :])
        mu = jnp.mean(v, axis=-1, keepdims=True)
        vc = v - mu
        var = jnp.mean(vc * vc, axis=-1, keepdims=True)
        vn = ((vc * lax.rsqrt(var + EPS)) * lnw_ref[...] + lnb_ref[...]).astype(BF16)
        for gp in range(GMLP_GROUPS // 2):
            va = vn[:, 2 * gp * gd:(2 * gp + 1) * gd]
            vb = vn[:, (2 * gp + 1) * gd:(2 * gp + 2) * gd]
            vz = jnp.zeros_like(va)
            rhs = jnp.concatenate([jnp.concatenate([va, vz], axis=1),
                                   jnp.concatenate([vz, vb], axis=1)], axis=0)
            sv2 = _dot(ws_pairs[gp], rhs)
            for half in range(2):
                g = 2 * gp + half
                sv = sv2[:, half * gd:(half + 1) * gd] + bst_ref[:, g:g + 1]
                ycat_s[r0:r0 + L, inner + g * gd:inner + (g + 1) * gd] = (
                    u[:, g * gd:(g + 1) * gd] * sv).astype(BF16)
            if gp % 2 == 1:
                fill()

    assert n_sub % MIXER_SLOTS == 0

    @pl.when(pl.program_id(1) == 0)
    def _sequence_start():
        state_s[...] = jnp.zeros_like(state_s)
        for task in project_tasks(0):
            task()

    for k in range(n_sub):
        if k >= 1:
            pending.extend(out_tasks(k - 1))
        pending.extend(project_tasks(k + 1))
        for ci in range(rb // L):
            chunk_work(k, ci)
        while pending:
            fill()
    for task in out_tasks(n_sub - 1):
        task()


def _const_spec(shape):
    nd = len(shape)
    return pl.BlockSpec(shape, lambda b, s: (0,) * nd, pipeline_mode=pl.Buffered(1))


def _mixer0(x, sh, sc, g, nw, in_w, conv_w, conv_b, dt_bias, a_log, d_skip, ssm_norm_w,
            ln_w, ln_b, w_s, b_s, out_w):
    bsz, seq, d = x.shape
    ts = SSD_TOKENS_PER_STEP
    rb = MIXER_ROWS_PER_SUBBLOCK
    inner = SSM_HEADS * SSM_HEAD_DIM
    conv_dim = inner + 2 * SSM_GROUPS * SSM_STATE
    ginner = w_s.shape[0] * w_s.shape[1]
    o1 = inner
    o2 = o1 + conv_dim
    o3 = o2 + SSM_HEADS
    w_in = jnp.concatenate(
        [in_w[:, :o2], in_w[:, o3:],
         jnp.pad(in_w[:, o2:o3], ((0, 0), (0, LANES - SSM_HEADS)))], axis=1).astype(BF16)
    pad_h = lambda a: jnp.pad(a.reshape(1, SSM_HEADS), ((0, 0), (0, LANES - SSM_HEADS)))
    e_np = np.zeros((LANES, inner), np.float32)
    for hh in range(SSM_HEADS):
        e_np[hh, hh * SSM_HEAD_DIM:(hh + 1) * SSM_HEAD_DIM] = 1.0
    consts = [
        nw.reshape(1, d), w_in,
        conv_w, conv_b.reshape(1, conv_dim), pad_h(dt_bias), pad_h(a_log),
        jnp.repeat(d_skip, SSM_HEAD_DIM).reshape(1, inner), ssm_norm_w.reshape(1, inner),
        ln_w.reshape(1, ginner), ln_b.reshape(1, ginner), w_s, b_s.T,
        out_w.astype(BF16), jnp.asarray(e_np, BF16),
    ]
    slot_scratch = [
        pltpu.VMEM((rb, d), BF16),
        pltpu.VMEM((conv_dim // LANES, rb + SUBLANES, LANES), F32),
        pltpu.VMEM((conv_dim // LANES, rb, LANES), F32),
        pltpu.VMEM((rb, inner), F32),
        pltpu.VMEM((rb, ginner), F32),
        pltpu.VMEM((rb, ginner), F32),
        pltpu.VMEM((rb, LANES), F32),
        pltpu.VMEM((rb, inner + ginner), BF16),
    ]
    row_spec = pl.BlockSpec((1, 1, d), lambda b, s: (b, 0, 0))
    x_spec = pl.BlockSpec((1, ts, d), lambda b, s: (b, s, 0))
    last_sub = seq // rb - 1
    xnext_spec = pl.BlockSpec(
        (1, rb, d), lambda b, s: (b, jnp.minimum((s + 1) * (ts // rb), last_sub), 0))
    return pl.pallas_call(
        _mixer0_kernel,
        grid=(bsz, seq // ts),
        in_specs=[x_spec, xnext_spec, row_spec, row_spec, row_spec]
        + [_const_spec(a.shape) for a in consts],
        out_specs=x_spec,
        out_shape=jax.ShapeDtypeStruct(x.shape, F32),
        scratch_shapes=[
            pltpu.VMEM((SSM_GROUPS, SSM_STATE, inner // SSM_GROUPS), F32),
        ] + slot_scratch * MIXER_SLOTS,
        compiler_params=pltpu.CompilerParams(
            dimension_semantics=("arbitrary", "arbitrary"),
            vmem_limit_bytes=VMEM_LIMIT_BYTES),
        name="mixer_ssd_gmlp",
    )(x, x, sh, sc, g, *consts)


def _folded_buckets():
    dist = (np.arange(CHUNK)[:, None] - np.arange(CHUNK)[None, :]) % CHUNK
    max_exact = REL_BUCKETS // 2
    log_ratio = (np.log(np.maximum(dist, 1).astype(np.float32) / max_exact)
                 / math.log(REL_MAX_DIST / max_exact))
    large = max_exact + (log_ratio * (REL_BUCKETS - max_exact)).astype(np.int32)
    return np.where(dist < max_exact, dist, np.minimum(large, REL_BUCKETS - 1)).astype(np.int32)


def _attn_kernel(sinks_ref, tab_ref, x_ref, sh_ref, sc_ref, g_ref, nw_ref,
                 wqkv_ref, bqkv_ref, wo_ref, bo_ref, bucket_ref,
                 o_ref,
                 bias_s, kprev_s, vprev_s, qkv_s, att_s):
    ts = x_ref.shape[1]
    L = CHUNK
    nq = ATTN_HEADS * ATTN_HEAD_DIM
    first_step = pl.program_id(1) == 0

    @pl.when(jnp.logical_and(pl.program_id(0) == 0, first_step))
    def _build_bias():
        bucket = bucket_ref[...]
        for hh in range(ATTN_HEADS):
            acc = jnp.zeros((L, L), F32)
            for b in range(REL_BUCKETS):
                acc = jnp.where(bucket == b, tab_ref[b * ATTN_HEADS + hh], acc)
            bias_s[hh] = acc

    @pl.when(first_step)
    def _init():
        kprev_s[...] = jnp.zeros_like(kprev_s)
        vprev_s[...] = jnp.zeros_like(vprev_s)

    x = x_ref[0]
    h = _norm_mod(x, nw_ref[...], sc_ref[0], sh_ref[0]).astype(BF16)
    qkv_s[...] = (_dot(h, wqkv_ref[...]) + bqkv_ref[...]).astype(BF16)

    row = lax.broadcasted_iota(jnp.int32, (L, L), 0)
    col = lax.broadcasted_iota(jnp.int32, (L, L), 1)
    upper = col > row
    lo_half = lax.broadcasted_iota(jnp.int32, (L, LANES), 1) < ATTN_HEAD_DIM
    no_prev = jnp.where(first_step, NEG_INF, 0.0)

    kw = ATTN_KV_HEADS * LANES
    for i in range(ts // L):
        r0 = i * L
        for kv in range(ATTN_KV_HEADS):
            k_cur = qkv_s[r0:r0 + L, nq + kv * LANES:nq + (kv + 1) * LANES]
            v_cur = qkv_s[r0:r0 + L, nq + kw + kv * LANES:nq + kw + (kv + 1) * LANES]
            if i == 0:
                k_prev = kprev_s[kv]
                v_prev = vprev_s[kv]
            else:
                k_prev = qkv_s[r0 - L:r0, nq + kv * LANES:nq + (kv + 1) * LANES]
                v_prev = qkv_s[r0 - L:r0, nq + kw + kv * LANES:nq + kw + (kv + 1) * LANES]
            kb = jnp.concatenate([k_prev, k_cur], axis=0)
            vb = jnp.concatenate([v_prev, v_cur], axis=0)
            for p in range(ATTN_HEADS // ATTN_KV_HEADS // 2):
                c0 = (kv * (ATTN_HEADS // ATTN_KV_HEADS // 2) + p) * LANES
                qp = qkv_s[r0:r0 + L, c0:c0 + LANES]
                zero = jnp.zeros_like(qp)
                outs = []
                for half in range(2):
                    hh = c0 // ATTN_HEAD_DIM + half
                    qm = jnp.where(lo_half, qp, zero) if half == 0 else jnp.where(lo_half, zero, qp)
                    l2 = _dot_nt(qm, kb)
                    lg = jnp.where(upper, l2[:, :L], l2[:, L:]) + bias_s[hh]
                    if i == 0:
                        lg = lg + jnp.where(upper, no_prev, 0.0)
                    sink = sinks_ref[hh]
                    m = jnp.maximum(jnp.max(lg, axis=-1, keepdims=True), sink)
                    e = jnp.exp(lg - m)
                    den = jnp.sum(e, axis=-1, keepdims=True) + jnp.exp(sink - m)
                    ez = jnp.zeros_like(e)
                    p2 = jnp.concatenate(
                        [jnp.where(upper, e, ez), jnp.where(upper, ez, e)], axis=1).astype(BF16)
                    outs.append(_dot(p2, vb) / den)
                att_s[r0:r0 + L, c0:c0 + LANES] = jnp.where(lo_half, outs[0], outs[1]).astype(BF16)
    for kv in range(ATTN_KV_HEADS):
        kprev_s[kv] = qkv_s[ts - L:ts, nq + kv * LANES:nq + (kv + 1) * LANES]
        vprev_s[kv] = qkv_s[ts - L:ts, nq + kw + kv * LANES:nq + kw + (kv + 1) * LANES]

    mix = _dot(att_s[...], wo_ref[...]) + bo_ref[...]
    o_ref[0] = x + g_ref[0] * mix


def _attn(x, sh, sc, g, nw, w_qkv, b_qkv, w_o, b_o, sinks, rel_table):
    bsz, seq, d = x.shape
    ts = TOKENS_PER_STEP
    nq = ATTN_HEADS * ATTN_HEAD_DIM
    nkv = ATTN_KV_HEADS * ATTN_HEAD_DIM
    scale = ATTN_HEAD_DIM ** -0.5

    def dup(a):
        lead = a.shape[:-1]
        a = a.reshape(lead + (ATTN_KV_HEADS, 1, ATTN_HEAD_DIM))
        return jnp.broadcast_to(a, lead + (ATTN_KV_HEADS, 2, ATTN_HEAD_DIM)).reshape(lead + (2 * nkv,))

    wq, wk, wv = w_qkv[:, :nq] * scale, w_qkv[:, nq:nq + nkv], w_qkv[:, nq + nkv:]
    bq, bk, bv = b_qkv[:nq] * scale, b_qkv[nq:nq + nkv], b_qkv[nq + nkv:]
    wqkv = jnp.concatenate([wq, dup(wk), dup(wv)], axis=1).astype(BF16)
    bqkv = jnp.concatenate([bq, dup(bk), dup(bv)]).reshape(1, -1)
    width = wqkv.shape[1]
    consts = [nw.reshape(1, d), wqkv, bqkv, w_o.astype(BF16), b_o.reshape(1, d),
              jnp.asarray(_folded_buckets())]
    row_spec = pl.BlockSpec((1, 1, d), lambda b, s: (b, 0, 0))
    x_spec = pl.BlockSpec((1, ts, d), lambda b, s: (b, s, 0))
    smem = pl.BlockSpec(memory_space=pltpu.SMEM)
    return pl.pallas_call(
        _attn_kernel,
        grid=(bsz, seq // ts),
        in_specs=[smem, smem, x_spec, row_spec, row_spec, row_spec]
        + [_const_spec(a.shape) for a in consts],
        out_specs=x_spec,
        out_shape=jax.ShapeDtypeStruct(x.shape, F32),
        scratch_shapes=[
            pltpu.VMEM((ATTN_HEADS, CHUNK, CHUNK), F32),
            pltpu.VMEM((ATTN_KV_HEADS, CHUNK, LANES), BF16),
            pltpu.VMEM((ATTN_KV_HEADS, CHUNK, LANES), BF16),
            pltpu.VMEM((ts, width), BF16),
            pltpu.VMEM((ts, nq), BF16),
        ],
        compiler_params=pltpu.CompilerParams(
            dimension_semantics=("arbitrary", "arbitrary"),
            vmem_limit_bytes=VMEM_LIMIT_BYTES),
        name="mixer_swa",
    )(sinks, rel_table.reshape(-1), x, sh, sc, g, *consts)


def _hidden_splits(hidden, max_cols):
    bounds, lo = [], 0
    while lo < hidden:
        hi = min(hidden, lo + max_cols)
        bounds.append((lo, hi))
        lo = hi
    return bounds


def _ffn_kernel(x_ref, sh_ref, sc_ref, g_ref, nw_ref, wg_ref, wu_ref, wd_ref, fnw_ref, o_ref,
                *, final_norm, splits):
    rows = x_ref.shape[1] // FFN_ROW_SUBBLOCKS
    for r in range(FFN_ROW_SUBBLOCKS):
        x = x_ref[0, r * rows:(r + 1) * rows, :]
        h = _norm_mod(x, nw_ref[...], sc_ref[0], sh_ref[0]).astype(BF16)
        acc = None
        for lo, hi in splits:
            a = _dot(h, wg_ref[0, :, lo:hi])
            b = _dot(h, wu_ref[0, :, lo:hi])
            part = _dot((_silu(a) * b).astype(BF16), wd_ref[0, lo:hi, :])
            acc = part if acc is None else acc + part
        out = x + g_ref[0] * acc
        if final_norm:
            out = _rms(out) * fnw_ref[...]
        o_ref[0, r * rows:(r + 1) * rows, :] = out


def _ffn(x, sh, sc, g, nw, layer, w_gate, w_up, w_down, final_w, final_norm):
    bsz, seq, d = x.shape
    ts = FFN_TOKENS_PER_STEP
    splits = _hidden_splits(w_gate.shape[2], FFN_HIDDEN_COLS_PER_PASS)
    row_spec = pl.BlockSpec((1, 1, d), lambda b, s: (b, 0, 0))
    x_spec = pl.BlockSpec((1, ts, d), lambda b, s: (b, s, 0))

    def layer_spec(a):
        return pl.BlockSpec((1,) + a.shape[1:], lambda b, s: (layer, 0, 0),
                            pipeline_mode=pl.Buffered(1))

    nw2, fw2 = nw.reshape(1, d), final_w.reshape(1, d)
    return pl.pallas_call(
        functools.partial(_ffn_kernel, final_norm=final_norm, splits=splits),
        grid=(bsz, seq // ts),
        in_specs=[x_spec, row_spec, row_spec, row_spec, _const_spec(nw2.shape),
                  layer_spec(w_gate), layer_spec(w_up), layer_spec(w_down), _const_spec(fw2.shape)],
        out_specs=x_spec,
        out_shape=jax.ShapeDtypeStruct(x.shape, F32),
        compiler_params=pltpu.CompilerParams(
            dimension_semantics=("parallel", "parallel"),
            vmem_limit_bytes=VMEM_LIMIT_BYTES),
        name="ffn_swiglu",
    )(x, sh, sc, g, nw2, w_gate, w_up, w_down, fw2)


def kernel(x, c, ada_w, ada_b, norm_mix_w, norm_ffn_w, in_w_even, conv_w, conv_b, dt_bias, a_log, d_skip, ssm_norm_w, gmlp_ln_w, gmlp_ln_b, gmlp_ws, gmlp_bs, out_w_even, qkv_w, qkv_b, o_w, o_b, sinks, rel_table, ffn_gate_w, ffn_up_w, ffn_down_w, final_norm_w):
    bsz, seq, d = x.shape
    depth = ada_w.shape[0]
    mods = _mods(c, ada_w, ada_b)
    gate_b, up_b, down_b = (w.astype(BF16) for w in (ffn_gate_w, ffn_up_w, ffn_down_w))
    for layer in range(depth):
        m = mods[layer].reshape(bsz, 6, 1, d)
        sh1, sc1, g1, sh2, sc2, g2 = (m[:, j] for j in range(6))
        i = layer // 2
        if layer % 2 == 0:
            x = _mixer0(x, sh1, sc1, g1, norm_mix_w[layer], in_w_even[i], conv_w[i], conv_b[i],
                        dt_bias[i], a_log[i], d_skip[i], ssm_norm_w[i], gmlp_ln_w[i], gmlp_ln_b[i],
                        gmlp_ws[i], gmlp_bs[i], out_w_even[i])
        else:
            x = _attn(x, sh1, sc1, g1, norm_mix_w[layer], qkv_w[i], qkv_b[i], o_w[i], o_b[i],
                      sinks[i], rel_table)
        x = _ffn(x, sh2, sc2, g2, norm_ffn_w[layer], layer, gate_b, up_b, down_b,
                 final_norm_w, final_norm=(layer == depth - 1))
    return x
```

```python
import functools
import math

import numpy as np
import jax
import jax.numpy as jnp
from jax import lax
from jax.experimental import pallas as pl
from jax.experimental.pallas import tpu as pltpu

F32 = jnp.float32
BF16 = jnp.bfloat16

EPS = 1e-6
NEG_INF = -1e30

SSM_HEADS = 16
SSM_HEAD_DIM = 64
SSM_GROUPS = 2
SSM_STATE = 128
SSM_CONV = 4
GMLP_GROUPS = 8
ATTN_HEADS = 16
ATTN_KV_HEADS = 2
ATTN_HEAD_DIM = 64
REL_BUCKETS = 32
REL_MAX_DIST = 128
CHUNK = 128

LANES = 128
SUBLANES = 8
VMEM_LIMIT_BYTES = 56 * 1024 * 1024

MXU_TILE = 256

TOKENS_PER_STEP = 1024
SSD_TOKENS_PER_STEP = 1024
MIXER_ROWS_PER_SUBBLOCK = 256
MIXER_SLOTS = 2
MIXER_COLS_PER_TASK = MXU_TILE
FFN_TOKENS_PER_STEP = 1024
FFN_HIDDEN_COLS_PER_PASS = 6 * MXU_TILE
FFN_ROW_SUBBLOCKS = 4
FFN_WEIGHT_CHUNKS = 8


def _sigmoid(x):
    return jax.nn.sigmoid(x)


def _silu(x):
    return x * _sigmoid(x)


def _gelu(x):
    return 0.5 * x * (1.0 + lax.erf(x * (1.0 / math.sqrt(2.0))))


def _softplus(x):
    return jnp.maximum(x, 0.0) + jnp.log(1.0 + jnp.exp(-jnp.abs(x)))


def _rms(x):
    return x * lax.rsqrt(jnp.mean(x * x, axis=-1, keepdims=True) + EPS)


def _norm_mod(x, nw, sc, sh):
    return (_rms(x) * nw) * (1.0 + sc) + sh


_dot = functools.partial(jnp.dot, preferred_element_type=F32)


def _dot_nt(a, b):
    return lax.dot_general(a, b, (((1,), (1,)), ((), ())), preferred_element_type=F32)


def _mods_kernel(ct_ref, w_ref, b_ref, o_ref):
    ct = ct_ref[...]
    cond = _silu(ct)
    w = w_ref[0]
    for b in range(ct.shape[1]):
        o_ref[0, b:b + 1, :] = jnp.sum(w * cond[:, b:b + 1], axis=0, keepdims=True) + b_ref[0]


def _mods(c, ada_w, ada_b):
    depth, d, n = ada_w.shape
    bsz = c.shape[0]
    tn = 1536
    return pl.pallas_call(
        _mods_kernel,
        grid=(depth, n // tn),
        in_specs=[
            pl.BlockSpec((d, bsz), lambda l, j: (0, 0)),
            pl.BlockSpec((1, d, tn), lambda l, j: (l, 0, j)),
            pl.BlockSpec((1, 1, tn), lambda l, j: (l, 0, j)),
        ],
        out_specs=pl.BlockSpec((1, bsz, tn), lambda l, j: (l, 0, j)),
        out_shape=jax.ShapeDtypeStruct((depth, bsz, n), F32),
        compiler_params=pltpu.CompilerParams(
            dimension_semantics=("arbitrary", "arbitrary"),
            vmem_limit_bytes=VMEM_LIMIT_BYTES),
        name="adaln_mods",
    )(c.T, ada_w, ada_b.reshape(depth, 1, n))


def _mixer0_kernel(x_ref, xnext_ref, sh_ref, sc_ref, g_ref, nw_ref,
                   win_ref,
                   cw_ref, cb_ref, dtb_ref, alog_ref, dsk_ref, snw_ref,
                   lnw_ref, lnb_ref, ws_ref, bst_ref, wout_ref, e_ref,
                   o_ref,
                   state_s, *slot_scratch):
    ts = x_ref.shape[1]
    L = CHUNK
    per_slot = len(slot_scratch) // MIXER_SLOTS
    slots = [slot_scratch[i * per_slot:(i + 1) * per_slot] for i in range(MIXER_SLOTS)]
    rb, d = slots[0][0].shape
    inner = slots[0][3].shape[1]
    tail_w = slots[0][4].shape[1]
    ginner = (tail_w - LANES) // 2
    gw = inner // SSM_GROUPS
    n_slab = slots[0][1].shape[0]
    n_xs = inner // LANES
    c_tail = inner + n_slab * LANES
    n_sub = ts // rb
    gd = ginner // GMLP_GROUPS

    row = lax.broadcasted_iota(jnp.int32, (L, L), 0)
    col = lax.broadcasted_iota(jnp.int32, (L, L), 1)
    causal = col <= row
    tril_b = jnp.where(causal, 1.0, 0.0).astype(BF16)
    lo_half = lax.broadcasted_iota(jnp.int32, (L, LANES), 1) < SSM_HEAD_DIM
    head_lane = lax.broadcasted_iota(jnp.int32, (1, LANES), 1) < SSM_HEADS
    a_neg = jnp.where(head_lane, -jnp.exp(alog_ref[...]), 0.0)
    ws_pairs = [
        jnp.concatenate([jnp.where(causal, ws_ref[2 * gp], 0.0),
                         jnp.where(causal, ws_ref[2 * gp + 1], 0.0)], axis=1).astype(BF16)
        for gp in range(GMLP_GROUPS // 2)]

    pending = []

    def fill():
        if pending:
            pending.pop(0)()

    def project_tasks(k):
        hb_s, xpad_s, _, z_s, tail_s, _ = slots[k % MIXER_SLOTS]

        def head():
            xk = xnext_ref[0] if k == n_sub else x_ref[0, k * rb:(k + 1) * rb, :]
            hb_s[...] = _norm_mod(xk, nw_ref[...], sc_ref[0], sh_ref[0]).astype(BF16)
            for j in range(n_slab):
                if k == 0:
                    xpad_s[j, 0:SUBLANES, :] = jnp.zeros((SUBLANES, LANES), F32)
                else:
                    prev_xpad = slots[(k - 1) % MIXER_SLOTS][1]
                    xpad_s[j, 0:SUBLANES, :] = prev_xpad[j, rb:rb + SUBLANES, :]

        def to_cols(dst, c_src, c0, c1):
            def run():
                dst[:, c0:c1] = _dot(hb_s[...], win_ref[:, c_src + c0:c_src + c1])
            return run

        def to_slabs(j0, j1):
            def run():
                r = _dot(hb_s[...], win_ref[:, inner + j0 * LANES:inner + j1 * LANES])
                for j in range(j0, j1):
                    xpad_s[j, SUBLANES:SUBLANES + rb, :] = r[:, (j - j0) * LANES:(j - j0 + 1) * LANES]
            return run

        step = MIXER_COLS_PER_TASK
        tasks = [head]
        tasks += [to_cols(z_s, 0, c0, c0 + step) for c0 in range(0, inner, step)]
        tasks += [to_slabs(j0, j0 + step // LANES) for j0 in range(0, n_slab, step // LANES)]
        tasks += [to_cols(tail_s, c_tail, c0, min(c0 + step, tail_w)) for c0 in range(0, tail_w, step)]
        return tasks

    def out_tasks(k):
        ycat_s = slots[k % MIXER_SLOTS][-1]
        rs = slice(k * rb, (k + 1) * rb)

        def cols(c0, c1):
            def run():
                mix = _dot(ycat_s[...], wout_ref[:, c0:c1])
                o_ref[0, rs, c0:c1] = x_ref[0, rs, c0:c1] + g_ref[0][:, c0:c1] * mix
            return run

        return [cols(c0, c0 + MXU_TILE) for c0 in range(0, d, MXU_TILE)]

    def chunk_work(k, ci):
        _, xpad_s, xact_s, z_s, tail_s, ycat_s = slots[k % MIXER_SLOTS]
        r0 = ci * L
        tail = tail_s[r0:r0 + L, :]
        dt_raw = tail[:, :LANES]
        uv = pltpu.roll(tail, tail_w - SSM_HEADS, 1)
        for j in range(n_slab):
            cw = cw_ref[:, j * LANES:(j + 1) * LANES]
            base = r0 + SUBLANES - (SSM_CONV - 1)
            acc = cb_ref[:, j * LANES:(j + 1) * LANES] + cw[0:1] * xpad_s[j, base:base + L, :]
            for t in range(1, SSM_CONV):
                acc = acc + cw[t:t + 1] * xpad_s[j, base + t:base + t + L, :]
            xact_s[j, r0:r0 + L, :] = _silu(acc)
        fill()
        xs = jnp.concatenate([xact_s[j, r0:r0 + L, :] for j in range(n_xs)], axis=1)

        dt = _softplus(dt_raw + dtb_ref[...])
        la = dt * a_neg
        la_hi = la.astype(BF16)
        la_lo = (la - la_hi.astype(F32)).astype(BF16)
        cum2 = _dot(tril_b, jnp.concatenate([la_hi, la_lo], axis=1))
        cum = cum2[:, :LANES] + cum2[:, LANES:]
        fill()
        cum_t = cum.T
        dt_t = dt.T
        cum_last = cum[L - 1:L, :]
        cdec = jnp.broadcast_to(jnp.exp(cum_last), (16, LANES))
        cdec_hi = cdec.astype(BF16)
        cdec_lo = (cdec - cdec_hi.astype(F32)).astype(BF16)
        small = jnp.concatenate(
            [jnp.exp(cum).astype(BF16),
             (dt * jnp.exp(cum_last - cum)).astype(BF16),
             cdec_hi, cdec_lo], axis=0)
        spread = _dot(small, e_ref[...])
        ecum_x = spread[0:L]
        w_x = spread[L:2 * L]
        cdec_x = spread[2 * L:2 * L + 1] + spread[2 * L + 16:2 * L + 17]
        fill()
        xsb = xs.astype(BF16)
        xdw = (xs * w_x).astype(BF16)

        y_parts = []
        for g in range(SSM_GROUPS):
            bg_t = xact_s[n_xs + g, r0:r0 + L, :].T.astype(BF16)
            cg = xact_s[n_xs + SSM_GROUPS + g, r0:r0 + L, :].astype(BF16)
            cb = _dot(cg, bg_t)
            prev = state_s[g]
            y_off = _dot(cg, prev.astype(BF16)) * ecum_x[:, g * gw:(g + 1) * gw]
            new = _dot(bg_t, xdw[:, g * gw:(g + 1) * gw])
            state_s[g] = prev * cdec_x[:, g * gw:(g + 1) * gw] + new
            fill()
            pairs = []
            for p in range(gw // LANES):
                h0 = g * (SSM_HEADS // SSM_GROUPS) + 2 * p
                ms = []
                for hh in (h0, h0 + 1):
                    seg = cum[:, hh:hh + 1] - cum_t[hh:hh + 1, :]
                    dec = jnp.exp(jnp.where(causal, seg, NEG_INF))
                    ms.append((cb * dec * dt_t[hh:hh + 1, :]).astype(BF16))
                lhs = jnp.concatenate(ms, axis=1)
                xp = xsb[:, h0 * SSM_HEAD_DIM:h0 * SSM_HEAD_DIM + LANES]
                zero = jnp.zeros_like(xp)
                rhs = jnp.concatenate(
                    [jnp.where(lo_half, xp, zero), jnp.where(lo_half, zero, xp)], axis=0)
                pairs.append(_dot(lhs, rhs))
                if p % 2 == 1:
                    fill()
            y_parts.append(jnp.concatenate(pairs, axis=1) + y_off)
        y = jnp.concatenate(y_parts, axis=1) + dsk_ref[...] * xs

        y = y * _silu(z_s[r0:r0 + L, :])
        y = jnp.concatenate([_rms(y[:, g * gw:(g + 1) * gw]) for g in range(SSM_GROUPS)], axis=1)
        ycat_s[r0:r0 + L, 0:inner] = (y * snw_ref[...]).astype(BF16)
        fill()

        u = _gelu(uv[:, :ginner])
        v = _gelu(uv[:, ginner:2 * ginner])
        mu = jnp.mean(v, axis=-1, keepdims=True)
        vc = v - mu
        var = jnp.mean(vc * vc, axis=-1, keepdims=True)
        vn = ((vc * lax.rsqrt(var + EPS)) * lnw_ref[...] + lnb_ref[...]).astype(BF16)
        for gp in range(GMLP_GROUPS // 2):
            va = vn[:, 2 * gp * gd:(2 * gp + 1) * gd]
            vb = vn[:, (2 * gp + 1) * gd:(2 * gp + 2) * gd]
            vz = jnp.zeros_like(va)
            rhs = jnp.concatenate([jnp.concatenate([va, vz], axis=1),
                                   jnp.concatenate([vz, vb], axis=1)], axis=0)
            sv2 = _dot(ws_pairs[gp], rhs)
            for half in range(2):
                g = 2 * gp + half
                sv = sv2[:, half * gd:(half + 1) * gd] + bst_ref[:, g:g + 1]
                ycat_s[r0:r0 + L, inner + g * gd:inner + (g + 1) * gd] = (
                    u[:, g * gd:(g + 1) * gd] * sv).astype(BF16)
            if gp % 2 == 1:
                fill()

    assert n_sub % MIXER_SLOTS == 0

    @pl.when(pl.program_id(1) == 0)
    def _sequence_start():
        state_s[...] = jnp.zeros_like(state_s)
        for task in project_tasks(0):
            task()

    for k in range(n_sub):
        if k >= 1:
            pending.extend(out_tasks(k - 1))
        pending.extend(project_tasks(k + 1))
        for ci in range(rb // L):
            chunk_work(k, ci)
        while pending:
            fill()
    for task in out_tasks(n_sub - 1):
        task()


def _const_spec(shape):
    nd = len(shape)
    return pl.BlockSpec(shape, lambda b, s: (0,) * nd, pipeline_mode=pl.Buffered(1))


def _mixer0(x, sh, sc, g, nw, in_w, conv_w, conv_b, dt_bias, a_log, d_skip, ssm_norm_w,
            ln_w, ln_b, w_s, b_s, out_w):
    bsz, seq, d = x.shape
    ts = SSD_TOKENS_PER_STEP
    rb = MIXER_ROWS_PER_SUBBLOCK
    inner = SSM_HEADS * SSM_HEAD_DIM
    conv_dim = inner + 2 * SSM_GROUPS * SSM_STATE
    ginner = w_s.shape[0] * w_s.shape[1]
    o1 = inner
    o2 = o1 + conv_dim
    o3 = o2 + SSM_HEADS
    w_in = jnp.pad(in_w, ((0, 0), (0, -in_w.shape[1] % LANES))).astype(BF16)
    tail_w = w_in.shape[1] - o2
    pad_h = lambda a: jnp.pad(a.reshape(1, SSM_HEADS), ((0, 0), (0, LANES - SSM_HEADS)))
    e_np = np.zeros((LANES, inner), np.float32)
    for hh in range(SSM_HEADS):
        e_np[hh, hh * SSM_HEAD_DIM:(hh + 1) * SSM_HEAD_DIM] = 1.0
    consts = [
        nw.reshape(1, d), w_in,
        conv_w, conv_b.reshape(1, conv_dim), pad_h(dt_bias), pad_h(a_log),
        jnp.repeat(d_skip, SSM_HEAD_DIM).reshape(1, inner), ssm_norm_w.reshape(1, inner),
        ln_w.reshape(1, ginner), ln_b.reshape(1, ginner), w_s, b_s.T,
        out_w.astype(BF16), jnp.asarray(e_np, BF16),
    ]
    slot_scratch = [
        pltpu.VMEM((rb, d), BF16),
        pltpu.VMEM((conv_dim // LANES, rb + SUBLANES, LANES), F32),
        pltpu.VMEM((conv_dim // LANES, rb, LANES), F32),
        pltpu.VMEM((rb, inner), F32),
        pltpu.VMEM((rb, tail_w), F32),
        pltpu.VMEM((rb, inner + ginner), BF16),
    ]
    row_spec = pl.BlockSpec((1, 1, d), lambda b, s: (b, 0, 0))
    x_spec = pl.BlockSpec((1, ts, d), lambda b, s: (b, s, 0))
    last_sub = seq // rb - 1
    xnext_spec = pl.BlockSpec(
        (1, rb, d), lambda b, s: (b, jnp.minimum((s + 1) * (ts // rb), last_sub), 0))
    return pl.pallas_call(
        _mixer0_kernel,
        grid=(bsz, seq // ts),
        in_specs=[x_spec, xnext_spec, row_spec, row_spec, row_spec]
        + [_const_spec(a.shape) for a in consts],
        out_specs=x_spec,
        out_shape=jax.ShapeDtypeStruct(x.shape, F32),
        scratch_shapes=[
            pltpu.VMEM((SSM_GROUPS, SSM_STATE, inner // SSM_GROUPS), F32),
        ] + slot_scratch * MIXER_SLOTS,
        compiler_params=pltpu.CompilerParams(
            dimension_semantics=("arbitrary", "arbitrary"),
            vmem_limit_bytes=VMEM_LIMIT_BYTES),
        name="mixer_ssd_gmlp",
    )(x, x, sh, sc, g, *consts)


def _folded_buckets():
    dist = (np.arange(CHUNK)[:, None] - np.arange(CHUNK)[None, :]) % CHUNK
    max_exact = REL_BUCKETS // 2
    log_ratio = (np.log(np.maximum(dist, 1).astype(np.float32) / max_exact)
                 / math.log(REL_MAX_DIST / max_exact))
    large = max_exact + (log_ratio * (REL_BUCKETS - max_exact)).astype(np.int32)
    return np.where(dist < max_exact, dist, np.minimum(large, REL_BUCKETS - 1)).astype(np.int32)


def _attn_kernel(sinks_ref, tab_ref, x_ref, sh_ref, sc_ref, g_ref, nw_ref,
                 wqkv_ref, bqkv_ref, wo_ref, bo_ref, bucket_ref,
                 o_ref,
                 bias_s, kprev_s, vprev_s, qkv_s, att_s):
    ts = x_ref.shape[1]
    L = CHUNK
    nq = ATTN_HEADS * ATTN_HEAD_DIM
    first_step = pl.program_id(1) == 0

    @pl.when(jnp.logical_and(pl.program_id(0) == 0, first_step))
    def _build_bias():
        bucket = bucket_ref[...]
        for hh in range(ATTN_HEADS):
            acc = jnp.zeros((L, L), F32)
            for b in range(REL_BUCKETS):
                acc = jnp.where(bucket == b, tab_ref[b * ATTN_HEADS + hh], acc)
            bias_s[hh] = acc

    @pl.when(first_step)
    def _init():
        kprev_s[...] = jnp.zeros_like(kprev_s)
        vprev_s[...] = jnp.zeros_like(vprev_s)

    x = x_ref[0]
    h = _norm_mod(x, nw_ref[...], sc_ref[0], sh_ref[0]).astype(BF16)
    qkv_s[...] = (_dot(h, wqkv_ref[...]) + bqkv_ref[...]).astype(BF16)

    row = lax.broadcasted_iota(jnp.int32, (L, L), 0)
    col = lax.broadcasted_iota(jnp.int32, (L, L), 1)
    upper = col > row
    lo_half = lax.broadcasted_iota(jnp.int32, (L, LANES), 1) < ATTN_HEAD_DIM
    no_prev = jnp.where(first_step, NEG_INF, 0.0)

    kw = ATTN_KV_HEADS * LANES
    for i in range(ts // L):
        r0 = i * L
        for kv in range(ATTN_KV_HEADS):
            k_cur = qkv_s[r0:r0 + L, nq + kv * LANES:nq + (kv + 1) * LANES]
            v_cur = qkv_s[r0:r0 + L, nq + kw + kv * LANES:nq + kw + (kv + 1) * LANES]
            if i == 0:
                k_prev = kprev_s[kv]
                v_prev = vprev_s[kv]
            else:
                k_prev = qkv_s[r0 - L:r0, nq + kv * LANES:nq + (kv + 1) * LANES]
                v_prev = qkv_s[r0 - L:r0, nq + kw + kv * LANES:nq + kw + (kv + 1) * LANES]
            kb = jnp.concatenate([k_prev, k_cur], axis=0)
            vb = jnp.concatenate([v_prev, v_cur], axis=0)
            for p in range(ATTN_HEADS // ATTN_KV_HEADS // 2):
                c0 = (kv * (ATTN_HEADS // ATTN_KV_HEADS // 2) + p) * LANES
                qp = qkv_s[r0:r0 + L, c0:c0 + LANES]
                zero = jnp.zeros_like(qp)
                outs = []
                for half in range(2):
                    hh = c0 // ATTN_HEAD_DIM + half
                    qm = jnp.where(lo_half, qp, zero) if half == 0 else jnp.where(lo_half, zero, qp)
                    l2 = _dot_nt(qm, kb)
                    lg = jnp.where(upper, l2[:, :L], l2[:, L:]) + bias_s[hh]
                    if i == 0:
                        lg = lg + jnp.where(upper, no_prev, 0.0)
                    sink = sinks_ref[hh]
                    m = jnp.maximum(jnp.max(lg, axis=-1, keepdims=True), sink)
                    e = jnp.exp(lg - m)
                    den = jnp.sum(e, axis=-1, keepdims=True) + jnp.exp(sink - m)
                    ez = jnp.zeros_like(e)
                    p2 = jnp.concatenate(
                        [jnp.where(upper, e, ez), jnp.where(upper, ez, e)], axis=1).astype(BF16)
                    outs.append(_dot(p2, vb) / den)
                att_s[r0:r0 + L, c0:c0 + LANES] = jnp.where(lo_half, outs[0], outs[1]).astype(BF16)
    for kv in range(ATTN_KV_HEADS):
        kprev_s[kv] = qkv_s[ts - L:ts, nq + kv * LANES:nq + (kv + 1) * LANES]
        vprev_s[kv] = qkv_s[ts - L:ts, nq + kw + kv * LANES:nq + kw + (kv + 1) * LANES]

    mix = _dot(att_s[...], wo_ref[...]) + bo_ref[...]
    o_ref[0] = x + g_ref[0] * mix


def _attn(x, sh, sc, g, nw, w_qkv, b_qkv, w_o, b_o, sinks, rel_table):
    bsz, seq, d = x.shape
    ts = TOKENS_PER_STEP
    nq = ATTN_HEADS * ATTN_HEAD_DIM
    nkv = ATTN_KV_HEADS * ATTN_HEAD_DIM
    scale = ATTN_HEAD_DIM ** -0.5

    def dup(a):
        lead = a.shape[:-1]
        a = a.reshape(lead + (ATTN_KV_HEADS, 1, ATTN_HEAD_DIM))
        return jnp.broadcast_to(a, lead + (ATTN_KV_HEADS, 2, ATTN_HEAD_DIM)).reshape(lead + (2 * nkv,))

    wq, wk, wv = w_qkv[:, :nq] * scale, w_qkv[:, nq:nq + nkv], w_qkv[:, nq + nkv:]
    bq, bk, bv = b_qkv[:nq] * scale, b_qkv[nq:nq + nkv], b_qkv[nq + nkv:]
    wqkv = jnp.concatenate([wq, dup(wk), dup(wv)], axis=1).astype(BF16)
    bqkv = jnp.concatenate([bq, dup(bk), dup(bv)]).reshape(1, -1)
    width = wqkv.shape[1]
    consts = [nw.reshape(1, d), wqkv, bqkv, w_o.astype(BF16), b_o.reshape(1, d),
              jnp.asarray(_folded_buckets())]
    row_spec = pl.BlockSpec((1, 1, d), lambda b, s: (b, 0, 0))
    x_spec = pl.BlockSpec((1, ts, d), lambda b, s: (b, s, 0))
    smem = pl.BlockSpec(memory_space=pltpu.SMEM)
    return pl.pallas_call(
        _attn_kernel,
        grid=(bsz, seq // ts),
        in_specs=[smem, smem, x_spec, row_spec, row_spec, row_spec]
        + [_const_spec(a.shape) for a in consts],
        out_specs=x_spec,
        out_shape=jax.ShapeDtypeStruct(x.shape, F32),
        scratch_shapes=[
            pltpu.VMEM((ATTN_HEADS, CHUNK, CHUNK), F32),
            pltpu.VMEM((ATTN_KV_HEADS, CHUNK, LANES), BF16),
            pltpu.VMEM((ATTN_KV_HEADS, CHUNK, LANES), BF16),
            pltpu.VMEM((ts, width), BF16),
            pltpu.VMEM((ts, nq), BF16),
        ],
        compiler_params=pltpu.CompilerParams(
            dimension_semantics=("arbitrary", "arbitrary"),
            vmem_limit_bytes=VMEM_LIMIT_BYTES),
        name="mixer_swa",
    )(sinks, rel_table.reshape(-1), x, sh, sc, g, *consts)


def _hidden_splits(hidden, max_cols):
    bounds, lo = [], 0
    while lo < hidden:
        hi = min(hidden, lo + max_cols)
        bounds.append((lo, hi))
        lo = hi
    return bounds


def _ffn_kernel(x_ref, sh_ref, sc_ref, g_ref, nw_ref, wg_ref, wu_ref, wd_ref, fnw_ref, o_ref,
                wg_s, wu_s, wd_s, *, final_norm, splits, n_load):
    t = pl.program_id(0)
    rows_in = wg_ref.shape[1]
    rows_dn = wd_ref.shape[1]

    for c in range(n_load):
        @pl.when(t == c)
        def _load(c=c):
            wg_s[c * rows_in:(c + 1) * rows_in, :] = wg_ref[0].astype(BF16)
            wu_s[c * rows_in:(c + 1) * rows_in, :] = wu_ref[0].astype(BF16)
            wd_s[c * rows_dn:(c + 1) * rows_dn, :] = wd_ref[0].astype(BF16)

    @pl.when(t >= n_load)
    def _compute():
        rows = x_ref.shape[1] // FFN_ROW_SUBBLOCKS

        def normed(r):
            xr = x_ref[0, r * rows:(r + 1) * rows, :]
            return _norm_mod(xr, nw_ref[...], sc_ref[0], sh_ref[0]).astype(BF16)

        h_next = normed(0)
        for r in range(FFN_ROW_SUBBLOCKS):
            x = x_ref[0, r * rows:(r + 1) * rows, :]
            h = h_next
            acc = None
            for i, (lo, hi) in enumerate(splits):
                a = _dot(h, wg_s[:, lo:hi])
                b = _dot(h, wu_s[:, lo:hi])
                if i == 0 and r + 1 < FFN_ROW_SUBBLOCKS:
                    h_next = normed(r + 1)
                part = _dot((_silu(a) * b).astype(BF16), wd_s[lo:hi, :])
                acc = part if acc is None else acc + part
            out = x + g_ref[0] * acc
            if final_norm:
                out = _rms(out) * fnw_ref[...]
            o_ref[0, r * rows:(r + 1) * rows, :] = out


def _ffn(x, sh, sc, g, nw, layer, w_gate, w_up, w_down, final_w, final_norm):
    bsz, seq, d = x.shape
    ts = FFN_TOKENS_PER_STEP
    hidden = w_gate.shape[2]
    n_load = FFN_WEIGHT_CHUNKS
    nblk = seq // ts
    splits = _hidden_splits(hidden, FFN_HIDDEN_COLS_PER_PASS)

    def blk(t):
        j = jnp.maximum(t - n_load, 0)
        return j // nblk, j % nblk

    row_spec = pl.BlockSpec((1, 1, d), lambda t: (blk(t)[0], 0, 0))
    x_spec = pl.BlockSpec((1, ts, d), lambda t: (blk(t)[0], blk(t)[1], 0))
    one = lambda shape: pl.BlockSpec(shape, lambda t: (0, 0), pipeline_mode=pl.Buffered(1))

    def chunk_spec(a):
        return pl.BlockSpec((1, a.shape[1] // n_load, a.shape[2]),
                            lambda t: (layer, jnp.minimum(t, n_load - 1), 0))

    nw2, fw2 = nw.reshape(1, d), final_w.reshape(1, d)
    return pl.pallas_call(
        functools.partial(_ffn_kernel, final_norm=final_norm, splits=splits, n_load=n_load),
        grid=(n_load + bsz * nblk,),
        in_specs=[x_spec, row_spec, row_spec, row_spec, one(nw2.shape),
                  chunk_spec(w_gate), chunk_spec(w_up), chunk_spec(w_down), one(fw2.shape)],
        out_specs=x_spec,
        out_shape=jax.ShapeDtypeStruct(x.shape, F32),
        scratch_shapes=[
            pltpu.VMEM((d, hidden), BF16),
            pltpu.VMEM((d, hidden), BF16),
            pltpu.VMEM((hidden, d), BF16),
        ],
        compiler_params=pltpu.CompilerParams(
            dimension_semantics=("arbitrary",),
            vmem_limit_bytes=VMEM_LIMIT_BYTES),
        name="ffn_swiglu",
    )(x, sh, sc, g, nw2, w_gate, w_up, w_down, fw2)


def kernel(x, c, ada_w, ada_b, norm_mix_w, norm_ffn_w, in_w_even, conv_w, conv_b, dt_bias, a_log, d_skip, ssm_norm_w, gmlp_ln_w, gmlp_ln_b, gmlp_ws, gmlp_bs, out_w_even, qkv_w, qkv_b, o_w, o_b, sinks, rel_table, ffn_gate_w, ffn_up_w, ffn_down_w, final_norm_w):
    bsz, seq, d = x.shape
    depth = ada_w.shape[0]
    mods = _mods(c, ada_w, ada_b)
    for layer in range(depth):
        m = mods[layer].reshape(bsz, 6, 1, d)
        sh1, sc1, g1, sh2, sc2, g2 = (m[:, j] for j in range(6))
        i = layer // 2
        if layer % 2 == 0:
            x = _mixer0(x, sh1, sc1, g1, norm_mix_w[layer], in_w_even[i], conv_w[i], conv_b[i],
                        dt_bias[i], a_log[i], d_skip[i], ssm_norm_w[i], gmlp_ln_w[i], gmlp_ln_b[i],
                        gmlp_ws[i], gmlp_bs[i], out_w_even[i])
        else:
            x = _attn(x, sh1, sc1, g1, norm_mix_w[layer], qkv_w[i], qkv_b[i], o_w[i], o_b[i],
                      sinks[i], rel_table)
        x = _ffn(x, sh2, sc2, g2, norm_ffn_w[layer], layer, ffn_gate_w, ffn_up_w, ffn_down_w,
                 final_norm_w, final_norm=(layer == depth - 1))
    return x
```

```python
import functools
import math

import numpy as np
import jax
import jax.numpy as jnp
from jax import lax
from jax.experimental import pallas as pl
from jax.experimental.pallas import tpu as pltpu

F32 = jnp.float32
BF16 = jnp.bfloat16

EPS = 1e-6
NEG_INF = -1e30

SSM_HEADS = 16
SSM_HEAD_DIM = 64
SSM_GROUPS = 2
SSM_STATE = 128
SSM_CONV = 4
GMLP_GROUPS = 8
ATTN_HEADS = 16
ATTN_KV_HEADS = 2
ATTN_HEAD_DIM = 64
REL_BUCKETS = 32
REL_MAX_DIST = 128
CHUNK = 128

LANES = 128
SUBLANES = 8
VMEM_LIMIT_BYTES = 56 * 1024 * 1024

MXU_TILE = 256

TOKENS_PER_STEP = 1024
SSD_TOKENS_PER_STEP = 1024
MIXER_ROWS_PER_SUBBLOCK = 256
MIXER_SLOTS = 2
MIXER_COLS_PER_TASK = MXU_TILE
FFN_TOKENS_PER_STEP = 1024
FFN_HIDDEN_COLS_PER_PASS = 6 * MXU_TILE
FFN_ROW_SUBBLOCKS = 4
FFN_WEIGHT_CHUNKS = 8


def _sigmoid(x):
    return jax.nn.sigmoid(x)


def _silu(x):
    return x * _sigmoid(x)


def _gelu(x):
    return 0.5 * x * (1.0 + lax.erf(x * (1.0 / math.sqrt(2.0))))


def _softplus(x):
    return jnp.maximum(x, 0.0) + jnp.log(1.0 + jnp.exp(-jnp.abs(x)))


def _rms(x):
    return x * lax.rsqrt(jnp.mean(x * x, axis=-1, keepdims=True) + EPS)


def _norm_mod(x, nw, sc, sh):
    return (_rms(x) * nw) * (1.0 + sc) + sh


_dot = functools.partial(jnp.dot, preferred_element_type=F32)


def _dot_nt(a, b):
    return lax.dot_general(a, b, (((1,), (1,)), ((), ())), preferred_element_type=F32)


def _mods_kernel(ct_ref, w_ref, b_ref, o_ref):
    ct = ct_ref[...]
    cond = _silu(ct)
    w = w_ref[0]
    for b in range(ct.shape[1]):
        o_ref[0, b:b + 1, :] = jnp.sum(w * cond[:, b:b + 1], axis=0, keepdims=True) + b_ref[0]


def _mods(c, ada_w, ada_b):
    depth, d, n = ada_w.shape
    bsz = c.shape[0]
    tn = 1536
    return pl.pallas_call(
        _mods_kernel,
        grid=(depth, n // tn),
        in_specs=[
            pl.BlockSpec((d, bsz), lambda l, j: (0, 0)),
            pl.BlockSpec((1, d, tn), lambda l, j: (l, 0, j)),
            pl.BlockSpec((1, 1, tn), lambda l, j: (l, 0, j)),
        ],
        out_specs=pl.BlockSpec((1, bsz, tn), lambda l, j: (l, 0, j)),
        out_shape=jax.ShapeDtypeStruct((depth, bsz, n), F32),
        compiler_params=pltpu.CompilerParams(
            dimension_semantics=("arbitrary", "arbitrary"),
            vmem_limit_bytes=VMEM_LIMIT_BYTES),
        name="adaln_mods",
    )(c.T, ada_w, ada_b.reshape(depth, 1, n))


def _mixer0_kernel(x_ref, xnext_ref, sh_ref, sc_ref, g_ref, nw_ref,
                   win_ref,
                   cw_ref, cb_ref, dtb_ref, alog_ref, dsk_ref, snw_ref,
                   lnw_ref, lnb_ref, ws_ref, bst_ref, wout_ref, e_ref,
                   o_ref,
                   state_s, *slot_scratch):
    ts = x_ref.shape[1]
    L = CHUNK
    per_slot = len(slot_scratch) // MIXER_SLOTS
    slots = [slot_scratch[i * per_slot:(i + 1) * per_slot] for i in range(MIXER_SLOTS)]
    rb, d = slots[0][0].shape
    inner = slots[0][3].shape[1]
    tail_w = slots[0][4].shape[1]
    ginner = (tail_w - LANES) // 2
    gw = inner // SSM_GROUPS
    n_slab = slots[0][1].shape[0]
    n_xs = inner // LANES
    c_tail = inner + n_slab * LANES
    n_sub = ts // rb
    gd = ginner // GMLP_GROUPS

    row = lax.broadcasted_iota(jnp.int32, (L, L), 0)
    col = lax.broadcasted_iota(jnp.int32, (L, L), 1)
    causal = col <= row
    tril_b = jnp.where(causal, 1.0, 0.0).astype(BF16)
    lo_half = lax.broadcasted_iota(jnp.int32, (L, LANES), 1) < SSM_HEAD_DIM
    head_lane = lax.broadcasted_iota(jnp.int32, (1, LANES), 1) < SSM_HEADS
    a_neg = jnp.where(head_lane, -jnp.exp(alog_ref[...]), 0.0)
    ws_pairs = [
        jnp.concatenate([jnp.where(causal, ws_ref[2 * gp], 0.0),
                         jnp.where(causal, ws_ref[2 * gp + 1], 0.0)], axis=1).astype(BF16)
        for gp in range(GMLP_GROUPS // 2)]

    pending = []

    def fill():
        if pending:
            pending.pop(0)()

    def project_tasks(k):
        hb_s, xpad_s, _, z_s, tail_s, _ = slots[k % MIXER_SLOTS]

        def head():
            xk = xnext_ref[0] if k == n_sub else x_ref[0, k * rb:(k + 1) * rb, :]
            hb_s[...] = _norm_mod(xk, nw_ref[...], sc_ref[0], sh_ref[0]).astype(BF16)
            for j in range(n_slab):
                if k == 0:
                    xpad_s[j, 0:SUBLANES, :] = jnp.zeros((SUBLANES, LANES), F32)
                else:
                    prev_xpad = slots[(k - 1) % MIXER_SLOTS][1]
                    xpad_s[j, 0:SUBLANES, :] = prev_xpad[j, rb:rb + SUBLANES, :]

        def to_cols(dst, c_src, c0, c1):
            def run():
                dst[:, c0:c1] = _dot(hb_s[...], win_ref[:, c_src + c0:c_src + c1])
            return run

        def to_slabs(j0, j1):
            def run():
                r = _dot(hb_s[...], win_ref[:, inner + j0 * LANES:inner + j1 * LANES])
                for j in range(j0, j1):
                    xpad_s[j, SUBLANES:SUBLANES + rb, :] = r[:, (j - j0) * LANES:(j - j0 + 1) * LANES]
            return run

        step = MIXER_COLS_PER_TASK
        tasks = [head]
        tasks += [to_cols(z_s, 0, c0, c0 + step) for c0 in range(0, inner, step)]
        tasks += [to_slabs(j0, j0 + step // LANES) for j0 in range(0, n_slab, step // LANES)]
        tasks += [to_cols(tail_s, c_tail, c0, min(c0 + step, tail_w)) for c0 in range(0, tail_w, step)]
        return tasks

    def out_tasks(k):
        ycat_s = slots[k % MIXER_SLOTS][-1]
        rs = slice(k * rb, (k + 1) * rb)

        def cols(c0, c1):
            def run():
                mix = _dot(ycat_s[...], wout_ref[:, c0:c1])
                o_ref[0, rs, c0:c1] = x_ref[0, rs, c0:c1] + g_ref[0][:, c0:c1] * mix
            return run

        return [cols(c0, c0 + MXU_TILE) for c0 in range(0, d, MXU_TILE)]

    def chunk_work(k, ci):
        _, xpad_s, xact_s, z_s, tail_s, ycat_s = slots[k % MIXER_SLOTS]
        r0 = ci * L
        tail = tail_s[r0:r0 + L, :]
        dt_raw = tail[:, :LANES]
        uv = pltpu.roll(tail, tail_w - SSM_HEADS, 1)
        for j in range(n_slab):
            cw = cw_ref[:, j * LANES:(j + 1) * LANES]
            base = r0 + SUBLANES - (SSM_CONV - 1)
            acc = cb_ref[:, j * LANES:(j + 1) * LANES] + cw[0:1] * xpad_s[j, base:base + L, :]
            for t in range(1, SSM_CONV):
                acc = acc + cw[t:t + 1] * xpad_s[j, base + t:base + t + L, :]
            xact_s[j, r0:r0 + L, :] = _silu(acc)
        fill()
        xs = jnp.concatenate([xact_s[j, r0:r0 + L, :] for j in range(n_xs)], axis=1)

        dt = _softplus(dt_raw + dtb_ref[...])
        la = dt * a_neg
        la_hi = la.astype(BF16)
        la_lo = (la - la_hi.astype(F32)).astype(BF16)
        cum2 = _dot(tril_b, jnp.concatenate([la_hi, la_lo], axis=1))
        cum = cum2[:, :LANES] + cum2[:, LANES:]
        fill()
        cum_t = cum.T
        dt_t = dt.T
        cum_last = cum[L - 1:L, :]
        cdec = jnp.broadcast_to(jnp.exp(cum_last), (16, LANES))
        cdec_hi = cdec.astype(BF16)
        cdec_lo = (cdec - cdec_hi.astype(F32)).astype(BF16)
        small = jnp.concatenate(
            [jnp.exp(cum).astype(BF16),
             (dt * jnp.exp(cum_last - cum)).astype(BF16),
             cdec_hi, cdec_lo], axis=0)
        spread = _dot(small, e_ref[...])
        ecum_x = spread[0:L]
        w_x = spread[L:2 * L]
        cdec_x = spread[2 * L:2 * L + 1] + spread[2 * L + 16:2 * L + 17]
        fill()
        xsb = xs.astype(BF16)
        xdw = (xs * w_x).astype(BF16)

        y_parts = []
        for g in range(SSM_GROUPS):
            bg_t = xact_s[n_xs + g, r0:r0 + L, :].T.astype(BF16)
            cg = xact_s[n_xs + SSM_GROUPS + g, r0:r0 + L, :].astype(BF16)
            cb = _dot(cg, bg_t)
            prev = state_s[g]
            y_off = _dot(cg, prev.astype(BF16)) * ecum_x[:, g * gw:(g + 1) * gw]
            new = _dot(bg_t, xdw[:, g * gw:(g + 1) * gw])
            state_s[g] = prev * cdec_x[:, g * gw:(g + 1) * gw] + new
            fill()
            pairs = []
            for p in range(gw // LANES):
                h0 = g * (SSM_HEADS // SSM_GROUPS) + 2 * p
                ms = []
                for hh in (h0, h0 + 1):
                    seg = cum[:, hh:hh + 1] - cum_t[hh:hh + 1, :]
                    dec = jnp.exp(jnp.where(causal, seg, NEG_INF))
                    ms.append((cb * dec * dt_t[hh:hh + 1, :]).astype(BF16))
                lhs = jnp.concatenate(ms, axis=1)
                xp = xsb[:, h0 * SSM_HEAD_DIM:h0 * SSM_HEAD_DIM + LANES]
                zero = jnp.zeros_like(xp)
                rhs = jnp.concatenate(
                    [jnp.where(lo_half, xp, zero), jnp.where(lo_half, zero, xp)], axis=0)
                pairs.append(_dot(lhs, rhs))
                if p % 2 == 1:
                    fill()
            y_parts.append(jnp.concatenate(pairs, axis=1) + y_off)
        y = jnp.concatenate(y_parts, axis=1) + dsk_ref[...] * xs

        y = y * _silu(z_s[r0:r0 + L, :])
        y = jnp.concatenate([_rms(y[:, g * gw:(g + 1) * gw]) for g in range(SSM_GROUPS)], axis=1)
        ycat_s[r0:r0 + L, 0:inner] = (y * snw_ref[...]).astype(BF16)
        fill()

        u = _gelu(uv[:, :ginner])
        v = _gelu(uv[:, ginner:2 * ginner])
        mu = jnp.mean(v, axis=-1, keepdims=True)
        vc = v - mu
        var = jnp.mean(vc * vc, axis=-1, keepdims=True)
        vn = ((vc * lax.rsqrt(var + EPS)) * lnw_ref[...] + lnb_ref[...]).astype(BF16)
        for gp in range(GMLP_GROUPS // 2):
            va = vn[:, 2 * gp * gd:(2 * gp + 1) * gd]
            vb = vn[:, (2 * gp + 1) * gd:(2 * gp + 2) * gd]
            vz = jnp.zeros_like(va)
            rhs = jnp.concatenate([jnp.concatenate([va, vz], axis=1),
                                   jnp.concatenate([vz, vb], axis=1)], axis=0)
            sv2 = _dot(ws_pairs[gp], rhs)
            for half in range(2):
                g = 2 * gp + half
                sv = sv2[:, half * gd:(half + 1) * gd] + bst_ref[:, g:g + 1]
                ycat_s[r0:r0 + L, inner + g * gd:inner + (g + 1) * gd] = (
                    u[:, g * gd:(g + 1) * gd] * sv).astype(BF16)
            if gp % 2 == 1:
                fill()

    assert n_sub % MIXER_SLOTS == 0

    @pl.when(pl.program_id(1) == 0)
    def _sequence_start():
        state_s[...] = jnp.zeros_like(state_s)
        for task in project_tasks(0):
            task()

    for k in range(n_sub):
        if k >= 1:
            pending.extend(out_tasks(k - 1))
        pending.extend(project_tasks(k + 1))
        for ci in range(rb // L):
            chunk_work(k, ci)
        while pending:
            fill()
    for task in out_tasks(n_sub - 1):
        task()


def _const_spec(shape):
    nd = len(shape)
    return pl.BlockSpec(shape, lambda b, s: (0,) * nd, pipeline_mode=pl.Buffered(1))


def _mixer0(x, sh, sc, g, nw, in_w, conv_w, conv_b, dt_bias, a_log, d_skip, ssm_norm_w,
            ln_w, ln_b, w_s, b_s, out_w):
    bsz, seq, d = x.shape
    ts = SSD_TOKENS_PER_STEP
    rb = MIXER_ROWS_PER_SUBBLOCK
    inner = SSM_HEADS * SSM_HEAD_DIM
    conv_dim = inner + 2 * SSM_GROUPS * SSM_STATE
    ginner = w_s.shape[0] * w_s.shape[1]
    o1 = inner
    o2 = o1 + conv_dim
    o3 = o2 + SSM_HEADS
    w_in = jnp.pad(in_w, ((0, 0), (0, -in_w.shape[1] % LANES))).astype(BF16)
    tail_w = w_in.shape[1] - o2
    pad_h = lambda a: jnp.pad(a.reshape(1, SSM_HEADS), ((0, 0), (0, LANES - SSM_HEADS)))
    e_np = np.zeros((LANES, inner), np.float32)
    for hh in range(SSM_HEADS):
        e_np[hh, hh * SSM_HEAD_DIM:(hh + 1) * SSM_HEAD_DIM] = 1.0
    consts = [
        nw.reshape(1, d), w_in,
        conv_w, conv_b.reshape(1, conv_dim), pad_h(dt_bias), pad_h(a_log),
        jnp.repeat(d_skip, SSM_HEAD_DIM).reshape(1, inner), ssm_norm_w.reshape(1, inner),
        ln_w.reshape(1, ginner), ln_b.reshape(1, ginner), w_s, b_s.T,
        out_w.astype(BF16), jnp.asarray(e_np, BF16),
    ]
    slot_scratch = [
        pltpu.VMEM((rb, d), BF16),
        pltpu.VMEM((conv_dim // LANES, rb + SUBLANES, LANES), F32),
        pltpu.VMEM((conv_dim // LANES, rb, LANES), F32),
        pltpu.VMEM((rb, inner), F32),
        pltpu.VMEM((rb, tail_w), F32),
        pltpu.VMEM((rb, inner + ginner), BF16),
    ]
    row_spec = pl.BlockSpec((1, 1, d), lambda b, s: (b, 0, 0))
    x_spec = pl.BlockSpec((1, ts, d), lambda b, s: (b, s, 0))
    last_sub = seq // rb - 1
    xnext_spec = pl.BlockSpec(
        (1, rb, d), lambda b, s: (b, jnp.minimum((s + 1) * (ts // rb), last_sub), 0))
    return pl.pallas_call(
        _mixer0_kernel,
        grid=(bsz, seq // ts),
        in_specs=[x_spec, xnext_spec, row_spec, row_spec, row_spec]
        + [_const_spec(a.shape) for a in consts],
        out_specs=x_spec,
        out_shape=jax.ShapeDtypeStruct(x.shape, F32),
        scratch_shapes=[
            pltpu.VMEM((SSM_GROUPS, SSM_STATE, inner // SSM_GROUPS), F32),
        ] + slot_scratch * MIXER_SLOTS,
        compiler_params=pltpu.CompilerParams(
            dimension_semantics=("arbitrary", "arbitrary"),
            vmem_limit_bytes=VMEM_LIMIT_BYTES,
            allow_input_fusion=[i in (6, 17) for i in range(5 + len(consts))]),
        name="mixer_ssd_gmlp",
    )(x, x, sh, sc, g, *consts)


def _folded_buckets():
    dist = (np.arange(CHUNK)[:, None] - np.arange(CHUNK)[None, :]) % CHUNK
    max_exact = REL_BUCKETS // 2
    log_ratio = (np.log(np.maximum(dist, 1).astype(np.float32) / max_exact)
                 / math.log(REL_MAX_DIST / max_exact))
    large = max_exact + (log_ratio * (REL_BUCKETS - max_exact)).astype(np.int32)
    return np.where(dist < max_exact, dist, np.minimum(large, REL_BUCKETS - 1)).astype(np.int32)


def _attn_kernel(sinks_ref, tab_ref, x_ref, sh_ref, sc_ref, g_ref, nw_ref,
                 wqkv_ref, bqkv_ref, wo_ref, bo_ref, bucket_ref,
                 o_ref,
                 bias_s, kprev_s, vprev_s, qkv_s, att_s):
    ts = x_ref.shape[1]
    L = CHUNK
    nq = ATTN_HEADS * ATTN_HEAD_DIM
    first_step = pl.program_id(1) == 0

    @pl.when(jnp.logical_and(pl.program_id(0) == 0, first_step))
    def _build_bias():
        bucket = bucket_ref[...]
        for hh in range(ATTN_HEADS):
            acc = jnp.zeros((L, L), F32)
            for b in range(REL_BUCKETS):
                acc = jnp.where(bucket == b, tab_ref[b * ATTN_HEADS + hh], acc)
            bias_s[hh] = acc

    @pl.when(first_step)
    def _init():
        kprev_s[...] = jnp.zeros_like(kprev_s)
        vprev_s[...] = jnp.zeros_like(vprev_s)

    x = x_ref[0]
    h = _norm_mod(x, nw_ref[...], sc_ref[0], sh_ref[0]).astype(BF16)
    qkv_s[...] = (_dot(h, wqkv_ref[...]) + bqkv_ref[...]).astype(BF16)

    row = lax.broadcasted_iota(jnp.int32, (L, L), 0)
    col = lax.broadcasted_iota(jnp.int32, (L, L), 1)
    upper = col > row
    lo_half = lax.broadcasted_iota(jnp.int32, (L, LANES), 1) < ATTN_HEAD_DIM
    no_prev = jnp.where(first_step, NEG_INF, 0.0)

    kw = ATTN_KV_HEADS * LANES
    for i in range(ts // L):
        r0 = i * L
        for kv in range(ATTN_KV_HEADS):
            k_cur = qkv_s[r0:r0 + L, nq + kv * LANES:nq + (kv + 1) * LANES]
            v_cur = qkv_s[r0:r0 + L, nq + kw + kv * LANES:nq + kw + (kv + 1) * LANES]
            if i == 0:
                k_prev = kprev_s[kv]
                v_prev = vprev_s[kv]
            else:
                k_prev = qkv_s[r0 - L:r0, nq + kv * LANES:nq + (kv + 1) * LANES]
                v_prev = qkv_s[r0 - L:r0, nq + kw + kv * LANES:nq + kw + (kv + 1) * LANES]
            kb = jnp.concatenate([k_prev, k_cur], axis=0)
            vb = jnp.concatenate([v_prev, v_cur], axis=0)
            for p in range(ATTN_HEADS // ATTN_KV_HEADS // 2):
                c0 = (kv * (ATTN_HEADS // ATTN_KV_HEADS // 2) + p) * LANES
                qp = qkv_s[r0:r0 + L, c0:c0 + LANES]
                zero = jnp.zeros_like(qp)
                outs = []
                for half in range(2):
                    hh = c0 // ATTN_HEAD_DIM + half
                    qm = jnp.where(lo_half, qp, zero) if half == 0 else jnp.where(lo_half, zero, qp)
                    l2 = _dot_nt(qm, kb)
                    lg = jnp.where(upper, l2[:, :L], l2[:, L:]) + bias_s[hh]
                    if i == 0:
                        lg = lg + jnp.where(upper, no_prev, 0.0)
                    sink = sinks_ref[hh]
                    m = jnp.maximum(jnp.max(lg, axis=-1, keepdims=True), sink)
                    e = jnp.exp(lg - m)
                    den = jnp.sum(e, axis=-1, keepdims=True) + jnp.exp(sink - m)
                    ez = jnp.zeros_like(e)
                    p2 = jnp.concatenate(
                        [jnp.where(upper, e, ez), jnp.where(upper, ez, e)], axis=1).astype(BF16)
                    outs.append(_dot(p2, vb) / den)
                att_s[r0:r0 + L, c0:c0 + LANES] = jnp.where(lo_half, outs[0], outs[1]).astype(BF16)
    for kv in range(ATTN_KV_HEADS):
        kprev_s[kv] = qkv_s[ts - L:ts, nq + kv * LANES:nq + (kv + 1) * LANES]
        vprev_s[kv] = qkv_s[ts - L:ts, nq + kw + kv * LANES:nq + kw + (kv + 1) * LANES]

    mix = _dot(att_s[...], wo_ref[...]) + bo_ref[...]
    o_ref[0] = x + g_ref[0] * mix


def _attn(x, sh, sc, g, nw, w_qkv, b_qkv, w_o, b_o, sinks, rel_table):
    bsz, seq, d = x.shape
    ts = TOKENS_PER_STEP
    nq = ATTN_HEADS * ATTN_HEAD_DIM
    nkv = ATTN_KV_HEADS * ATTN_HEAD_DIM
    scale = ATTN_HEAD_DIM ** -0.5

    def dup(a):
        lead = a.shape[:-1]
        a = a.reshape(lead + (ATTN_KV_HEADS, 1, ATTN_HEAD_DIM))
        return jnp.broadcast_to(a, lead + (ATTN_KV_HEADS, 2, ATTN_HEAD_DIM)).reshape(lead + (2 * nkv,))

    wq, wk, wv = w_qkv[:, :nq] * scale, w_qkv[:, nq:nq + nkv], w_qkv[:, nq + nkv:]
    bq, bk, bv = b_qkv[:nq] * scale, b_qkv[nq:nq + nkv], b_qkv[nq + nkv:]
    wqkv = jnp.concatenate([wq, dup(wk), dup(wv)], axis=1).astype(BF16)
    bqkv = jnp.concatenate([bq, dup(bk), dup(bv)]).reshape(1, -1)
    width = wqkv.shape[1]
    consts = [nw.reshape(1, d), wqkv, bqkv, w_o.astype(BF16), b_o.reshape(1, d),
              jnp.asarray(_folded_buckets())]
    row_spec = pl.BlockSpec((1, 1, d), lambda b, s: (b, 0, 0))
    x_spec = pl.BlockSpec((1, ts, d), lambda b, s: (b, s, 0))
    smem = pl.BlockSpec(memory_space=pltpu.SMEM)
    return pl.pallas_call(
        _attn_kernel,
        grid=(bsz, seq // ts),
        in_specs=[smem, smem, x_spec, row_spec, row_spec, row_spec]
        + [_const_spec(a.shape) for a in consts],
        out_specs=x_spec,
        out_shape=jax.ShapeDtypeStruct(x.shape, F32),
        scratch_shapes=[
            pltpu.VMEM((ATTN_HEADS, CHUNK, CHUNK), F32),
            pltpu.VMEM((ATTN_KV_HEADS, CHUNK, LANES), BF16),
            pltpu.VMEM((ATTN_KV_HEADS, CHUNK, LANES), BF16),
            pltpu.VMEM((ts, width), BF16),
            pltpu.VMEM((ts, nq), BF16),
        ],
        compiler_params=pltpu.CompilerParams(
            dimension_semantics=("arbitrary", "arbitrary"),
            vmem_limit_bytes=VMEM_LIMIT_BYTES,
            allow_input_fusion=[i in (7, 9) for i in range(6 + len(consts))]),
        name="mixer_swa",
    )(sinks, rel_table.reshape(-1), x, sh, sc, g, *consts)


def _hidden_splits(hidden, max_cols):
    bounds, lo = [], 0
    while lo < hidden:
        hi = min(hidden, lo + max_cols)
        bounds.append((lo, hi))
        lo = hi
    return bounds


def _ffn_kernel(x_ref, sh_ref, sc_ref, g_ref, nw_ref, wg_ref, wu_ref, wd_ref, fnw_ref, o_ref,
                wg_s, wu_s, wd_s, *, final_norm, splits, n_load):
    t = pl.program_id(0)
    rows_in = wg_ref.shape[1]
    rows_dn = wd_ref.shape[1]

    for c in range(n_load):
        @pl.when(t == c)
        def _load(c=c):
            wg_s[c * rows_in:(c + 1) * rows_in, :] = wg_ref[0].astype(BF16)
            wu_s[c * rows_in:(c + 1) * rows_in, :] = wu_ref[0].astype(BF16)
            wd_s[c * rows_dn:(c + 1) * rows_dn, :] = wd_ref[0].astype(BF16)

    @pl.when(t >= n_load)
    def _compute():
        rows = x_ref.shape[1] // FFN_ROW_SUBBLOCKS
        for r in range(FFN_ROW_SUBBLOCKS):
            x = x_ref[0, r * rows:(r + 1) * rows, :]
            h = _norm_mod(x, nw_ref[...], sc_ref[0], sh_ref[0]).astype(BF16)
            acc = None
            for lo, hi in splits:
                a = _dot(h, wg_s[:, lo:hi])
                b = _dot(h, wu_s[:, lo:hi])
                part = _dot((_silu(a) * b).astype(BF16), wd_s[lo:hi, :])
                acc = part if acc is None else acc + part
            out = x + g_ref[0] * acc
            if final_norm:
                out = _rms(out) * fnw_ref[...]
            o_ref[0, r * rows:(r + 1) * rows, :] = out


def _ffn(x, sh, sc, g, nw, layer, w_gate, w_up, w_down, final_w, final_norm):
    bsz, seq, d = x.shape
    ts = FFN_TOKENS_PER_STEP
    hidden = w_gate.shape[2]
    n_load = FFN_WEIGHT_CHUNKS
    nblk = seq // ts
    splits = _hidden_splits(hidden, FFN_HIDDEN_COLS_PER_PASS)

    def blk(t):
        j = jnp.maximum(t - n_load, 0)
        return j // nblk, j % nblk

    row_spec = pl.BlockSpec((1, 1, d), lambda t: (blk(t)[0], 0, 0))
    x_spec = pl.BlockSpec((1, ts, d), lambda t: (blk(t)[0], blk(t)[1], 0))
    one = lambda shape: pl.BlockSpec(shape, lambda t: (0, 0), pipeline_mode=pl.Buffered(1))

    def chunk_spec(a):
        return pl.BlockSpec((1, a.shape[1] // n_load, a.shape[2]),
                            lambda t: (layer, jnp.minimum(t, n_load - 1), 0))

    nw2, fw2 = nw.reshape(1, d), final_w.reshape(1, d)
    return pl.pallas_call(
        functools.partial(_ffn_kernel, final_norm=final_norm, splits=splits, n_load=n_load),
        grid=(n_load + bsz * nblk,),
        in_specs=[x_spec, row_spec, row_spec, row_spec, one(nw2.shape),
                  chunk_spec(w_gate), chunk_spec(w_up), chunk_spec(w_down), one(fw2.shape)],
        out_specs=x_spec,
        out_shape=jax.ShapeDtypeStruct(x.shape, F32),
        scratch_shapes=[
            pltpu.VMEM((d, hidden), BF16),
            pltpu.VMEM((d, hidden), BF16),
            pltpu.VMEM((hidden, d), BF16),
        ],
        compiler_params=pltpu.CompilerParams(
            dimension_semantics=("arbitrary",),
            vmem_limit_bytes=VMEM_LIMIT_BYTES),
        name="ffn_swiglu",
    )(x, sh, sc, g, nw2, w_gate, w_up, w_down, fw2)


def kernel(x, c, ada_w, ada_b, norm_mix_w, norm_ffn_w, in_w_even, conv_w, conv_b, dt_bias, a_log, d_skip, ssm_norm_w, gmlp_ln_w, gmlp_ln_b, gmlp_ws, gmlp_bs, out_w_even, qkv_w, qkv_b, o_w, o_b, sinks, rel_table, ffn_gate_w, ffn_up_w, ffn_down_w, final_norm_w):
    bsz, seq, d = x.shape
    depth = ada_w.shape[0]
    mods = _mods(c, ada_w, ada_b)
    for layer in range(depth):
        m = mods[layer].reshape(bsz, 6, 1, d)
        sh1, sc1, g1, sh2, sc2, g2 = (m[:, j] for j in range(6))
        i = layer // 2
        if layer % 2 == 0:
            x = _mixer0(x, sh1, sc1, g1, norm_mix_w[layer], in_w_even[i], conv_w[i], conv_b[i],
                        dt_bias[i], a_log[i], d_skip[i], ssm_norm_w[i], gmlp_ln_w[i], gmlp_ln_b[i],
                        gmlp_ws[i], gmlp_bs[i], out_w_even[i])
        else:
            x = _attn(x, sh1, sc1, g1, norm_mix_w[layer], qkv_w[i], qkv_b[i], o_w[i], o_b[i],
                      sinks[i], rel_table)
        x = _ffn(x, sh2, sc2, g2, norm_ffn_w[layer], layer, ffn_gate_w, ffn_up_w, ffn_down_w,
                 final_norm_w, final_norm=(layer == depth - 1))
    return x
```
